```python
import math
import jax, jax.numpy as jnp
from jax import lax
import numpy as np

D_MODEL = 1024
BATCH = 16
SEQ = 2048
DEPTH = 1

GRID_W = 64
ROPE_THETA = 10000.0
Q_BLOCK = 128
EPS = 1e-6

MLA_HEADS = 8
Q_LORA = 384
KV_LORA = 256
MLA_NOPE = 64
MLA_ROPE = 32
MLA_V = 64
MLA_QK = MLA_NOPE + MLA_ROPE

GQA_HEADS = 8
GQA_KV_HEADS = 2
GQA_HD = 64

IN_SPLITS = (Q_LORA, KV_LORA, MLA_ROPE, GQA_HEADS * GQA_HD, GQA_KV_HEADS * GQA_HD, GQA_KV_HEADS * GQA_HD)
IN_WIDTH = sum(IN_SPLITS)
MLA_OUT = MLA_HEADS * MLA_V
GQA_OUT = GQA_HEADS * GQA_HD
MIX_WIDTH = MLA_OUT + GQA_OUT

D_FF = int(math.ceil((8 * D_MODEL / 3) / 256) * 256)

kernel_name = "hybrid_mla_gqa_axial_encoder_block"


def rmsnorm(x, g):
    xf = x.astype(jnp.float32)
    y = xf * lax.rsqrt(jnp.mean(xf * xf, axis=-1, keepdims=True) + EPS)
    return (y * g.astype(jnp.float32)).astype(x.dtype)


def axial_angles(rows, d_rot):
    d_ax = d_rot // 2
    inv = ROPE_THETA ** (-(jnp.arange(0, d_ax, 2, dtype=jnp.float32) / d_ax))
    row = jnp.repeat(jnp.arange(rows, dtype=jnp.float32), GRID_W)
    col = jnp.tile(jnp.arange(GRID_W, dtype=jnp.float32), rows)
    return row[:, None] * inv[None, :], col[:, None] * inv[None, :]


def rotate(x, ang):
    n = x.shape[-1] // 2
    c = jnp.cos(ang)[:, None, :].astype(x.dtype)
    s = jnp.sin(ang)[:, None, :].astype(x.dtype)
    x1, x2 = x[..., :n], x[..., n:]
    return jnp.concatenate([x1 * c - x2 * s, x1 * s + x2 * c], axis=-1)


def axial_rope(x, ang_r, ang_c):
    d_ax = x.shape[-1] // 2
    return jnp.concatenate([rotate(x[..., :d_ax], ang_r), rotate(x[..., d_ax:], ang_c)], axis=-1)


def blocked_attention(q, k, v, scale):
    B, S, H, D = q.shape
    Hk = k.shape[2]
    G = H // Hk
    Dv = v.shape[-1]
    nb = S // Q_BLOCK
    kf = k.astype(jnp.float32)
    vf = v.astype(jnp.float32)
    qb = q.reshape(B, nb, Q_BLOCK, Hk, G, D).transpose(1, 0, 3, 4, 2, 5)

    def one_block(qblk):
        s = jnp.einsum('bkgqd,bskd->bkgqs', qblk.astype(jnp.float32), kf) * scale
        p = jax.nn.softmax(s, axis=-1)
        return jnp.einsum('bkgqs,bskd->bkgqd', p, vf)

    o = lax.map(one_block, qb)
    o = o.transpose(1, 0, 4, 2, 3, 5).reshape(B, S, H * Dv)
    return o.astype(q.dtype)


def setup_inputs(seed: int = 0) -> dict:
    key = jax.random.key(seed)
    ks = jax.random.split(key, 20)
    L = DEPTH

    def w(k, shape, fan_in):
        return jax.random.normal(k, shape, jnp.float32) * (fan_in ** -0.5)

    def gain(k, n):
        return 1.0 + 0.02 * jax.random.normal(k, (L, n), jnp.float32)

    return {
        "x": jax.random.normal(ks[0], (BATCH, SEQ, D_MODEL), jnp.float32),
        "norm1_g": gain(ks[1], D_MODEL),
        "w_in": w(ks[2], (L, D_MODEL, IN_WIDTH), D_MODEL),
        "q_a_norm_g": gain(ks[3], Q_LORA),
        "w_q_b": w(ks[4], (L, Q_LORA, MLA_HEADS * MLA_QK), Q_LORA),
        "kv_a_norm_g": gain(ks[5], KV_LORA),
        "w_kv_b": w(ks[6], (L, KV_LORA, MLA_HEADS * (MLA_NOPE + MLA_V)), KV_LORA),
        "mla_q_norm_g": gain(ks[7], MLA_QK),
        "mla_k_norm_g": gain(ks[8], MLA_QK),
        "gqa_q_norm_g": gain(ks[9], GQA_HD),
        "gqa_k_norm_g": gain(ks[10], GQA_HD),
        "mla_out_norm_g": gain(ks[11], MLA_OUT),
        "gqa_out_norm_g": gain(ks[12], GQA_OUT),
        "w_o": w(ks[13], (L, MIX_WIDTH, D_MODEL), MIX_WIDTH),
        "norm2_g": gain(ks[14], D_MODEL),
        "w_gate": w(ks[15], (L, D_MODEL, D_FF), D_MODEL),
        "w_up": w(ks[16], (L, D_MODEL, D_FF), D_MODEL),
        "w_down": w(ks[17], (L, D_FF, D_MODEL), D_FF),
    }


def reference(x, norm1_g, w_in, q_a_norm_g, w_q_b, kv_a_norm_g, w_kv_b,
              mla_q_norm_g, mla_k_norm_g, gqa_q_norm_g, gqa_k_norm_g,
              mla_out_norm_g, gqa_out_norm_g, w_o, norm2_g, w_gate, w_up, w_down):
    B, S, _ = x.shape
    rows = S // GRID_W
    ang_r_mla, ang_c_mla = axial_angles(rows, MLA_ROPE)
    ang_r_gqa, ang_c_gqa = axial_angles(rows, GQA_HD)
    offsets = list(np.cumsum(IN_SPLITS)[:-1])

    for l in range(DEPTH):
        h = rmsnorm(x, norm1_g[l])
        p = h @ w_in[l]
        cq, ckv, kpe, gq, gk, gv = jnp.split(p, offsets, axis=-1)

        cq = rmsnorm(cq, q_a_norm_g[l])
        q_a = (cq @ w_q_b[l]).reshape(B, S, MLA_HEADS, MLA_QK)
        ckv = rmsnorm(ckv, kv_a_norm_g[l])
        kv = (ckv @ w_kv_b[l]).reshape(B, S, MLA_HEADS, MLA_NOPE + MLA_V)
        k_nope, v_a = kv[..., :MLA_NOPE], kv[..., MLA_NOPE:]
        k_pe = jnp.broadcast_to(kpe[:, :, None, :], (B, S, MLA_HEADS, MLA_ROPE))
        k_a = jnp.concatenate([k_nope, k_pe], axis=-1)
        q_a = rmsnorm(q_a, mla_q_norm_g[l])
        k_a = rmsnorm(k_a, mla_k_norm_g[l])
        q_a = jnp.concatenate([q_a[..., :MLA_NOPE], axial_rope(q_a[..., MLA_NOPE:], ang_r_mla, ang_c_mla)], axis=-1)
        k_a = jnp.concatenate([k_a[..., :MLA_NOPE], axial_rope(k_a[..., MLA_NOPE:], ang_r_mla, ang_c_mla)], axis=-1)
        o_a = blocked_attention(q_a, k_a, v_a, MLA_QK ** -0.5)

        q_b = rmsnorm(gq.reshape(B, S, GQA_HEADS, GQA_HD), gqa_q_norm_g[l])
        k_b = rmsnorm(gk.reshape(B, S, GQA_KV_HEADS, GQA_HD), gqa_k_norm_g[l])
        v_b = gv.reshape(B, S, GQA_KV_HEADS, GQA_HD)
        q_b = axial_rope(q_b, ang_r_gqa, ang_c_gqa)
        k_b = axial_rope(k_b, ang_r_gqa, ang_c_gqa)
        o_b = blocked_attention(q_b, k_b, v_b, GQA_HD ** -0.5)

        mixed = jnp.concatenate([rmsnorm(o_a, mla_out_norm_g[l]), rmsnorm(o_b, gqa_out_norm_g[l])], axis=-1)
        x = x + mixed @ w_o[l]

        h2 = rmsnorm(x, norm2_g[l])
        x = x + (jax.nn.silu(h2 @ w_gate[l]) * (h2 @ w_up[l])) @ w_down[l]
    return x
```

```python
import functools
import math

import jax
import jax.numpy as jnp
from jax import lax
from jax.experimental import pallas as pl
from jax.experimental.pallas import tpu as pltpu

D_MODEL = 1024
GRID_W = 64
ROPE_THETA = 10000.0
EPS = 1e-6

MLA_HEADS = 8
Q_LORA = 384
KV_LORA = 256
MLA_NOPE = 64
MLA_ROPE = 32
MLA_V = 64
MLA_QK = MLA_NOPE + MLA_ROPE

GQA_HEADS = 8
GQA_KV_HEADS = 2
GQA_HD = 64
GQA_GROUP = GQA_HEADS // GQA_KV_HEADS

D_FF = 2816

LANES = 128
V_HALF = 64
LOG2E = math.log2(math.e)
VMEM_LIMIT_BYTES = 56 * 1024 * 1024

TM_PROJ = 512
TQ_ATTN = 512
TM_FFN = 512
MXU_DEPTH = 256
FFN_BOUNDS = (0, D_FF)

_P_CQ = (0, Q_LORA)
_P_CKV = (_P_CQ[1], _P_CQ[1] + KV_LORA)
_P_KPE = (_P_CKV[1], _P_CKV[1] + LANES)
_P_GQ = (_P_KPE[1], _P_KPE[1] + GQA_HEADS * GQA_HD)
_P_GK = (_P_GQ[1], _P_GQ[1] + LANES)
_P_GV = (_P_GK[1], _P_GK[1] + LANES)
P_WIDTH = _P_GV[1]


def _bf16_dot(a, b):
    return jnp.dot(a.astype(jnp.bfloat16), b, preferred_element_type=jnp.float32)


def _rms_scale(x, width):
    return lax.rsqrt(jnp.sum(x * x, axis=-1, keepdims=True) * (1.0 / width) + EPS)


def _rope(x, c, sa, sb, shift):
    return x * c + pltpu.roll(x, LANES - shift, 1) * sa + pltpu.roll(x, shift, 1) * sb


def _proj_kernel(x_ref, g1_ref, win_ref, gcq_ref, wqb_ref, gckv_ref, wkvb_ref,
                 gmq_ref, gmkn_ref, gmkp_ref, ggq_ref, ggk_ref,
                 cm_ref, sam_ref, sbm_ref, cg_ref, sag_ref, sbg_ref,
                 qa_ref, ka_ref, va_ref, qb_ref, kb_ref, vb_ref):
    x = x_ref[...]
    h = x * _rms_scale(x, D_MODEL) * g1_ref[...]
    p = _bf16_dot(h, win_ref[...])

    cq = p[:, _P_CQ[0]:_P_CQ[1]]
    ckv = p[:, _P_CKV[0]:_P_CKV[1]]
    kpe = p[:, _P_KPE[0]:_P_KPE[1]]
    q_lat = _bf16_dot(cq * _rms_scale(cq, Q_LORA) * gcq_ref[...], wqb_ref[...])
    kv = _bf16_dot(ckv * _rms_scale(ckv, KV_LORA) * gckv_ref[...], wkvb_ref[...])

    cm, sam, sbm = cm_ref[...], sam_ref[...], sbm_ref[...]
    gmq = gmq_ref[...] * (MLA_QK ** -0.5 * LOG2E)
    gmkn = gmkn_ref[...]
    kpe_rot = _rope(kpe * gmkp_ref[...], cm, sam, sbm, MLA_ROPE // 4)
    ss_pe = jnp.sum(kpe * kpe, axis=-1, keepdims=True)
    for hh in range(MLA_HEADS):
        sl = slice(hh * LANES, (hh + 1) * LANES)
        qh = q_lat[:, sl]
        qn = qh * _rms_scale(qh, MLA_QK) * gmq
        qa_ref[:, sl] = _rope(qn, cm, sam, sbm, MLA_ROPE // 4).astype(qa_ref.dtype)
        kn = kv[:, sl]
        r = lax.rsqrt((jnp.sum(kn * kn, axis=-1, keepdims=True) + ss_pe) * (1.0 / MLA_QK) + EPS)
        ka_ref[:, sl] = ((kn * gmkn + kpe_rot) * r).astype(ka_ref.dtype)
    va_ref[...] = kv[:, MLA_HEADS * LANES:].astype(va_ref.dtype)

    cg, sag, sbg = cg_ref[...], sag_ref[...], sbg_ref[...]
    lo = lax.broadcasted_iota(jnp.int32, (x.shape[0], LANES), 1) < GQA_HD

    def pair_norm(seg):
        sq = seg * seg
        ss_lo = jnp.sum(jnp.where(lo, sq, 0.0), axis=-1, keepdims=True)
        ss_hi = jnp.sum(jnp.where(lo, 0.0, sq), axis=-1, keepdims=True)
        r_lo = lax.rsqrt(ss_lo * (1.0 / GQA_HD) + EPS)
        r_hi = lax.rsqrt(ss_hi * (1.0 / GQA_HD) + EPS)
        return seg * jnp.where(lo, r_lo, r_hi)

    ggq = ggq_ref[...] * (GQA_HD ** -0.5 * LOG2E)
    for s in range(GQA_GROUP):
        sl = slice(s * LANES, (s + 1) * LANES)
        seg = p[:, _P_GQ[0] + s * LANES:_P_GQ[0] + (s + 1) * LANES]
        qb_ref[:, sl] = _rope(pair_norm(seg) * ggq, cg, sag, sbg, GQA_HD // 4).astype(qb_ref.dtype)
    gk = p[:, _P_GK[0]:_P_GK[1]]
    kb_ref[...] = _rope(pair_norm(gk) * ggk_ref[...], cg, sag, sbg, GQA_HD // 4).astype(kb_ref.dtype)
    vb_ref[...] = p[:, _P_GV[0]:_P_GV[1]].astype(vb_ref.dtype)


def _const_spec(shape):
    return pl.BlockSpec(shape, lambda *_: (0,) * len(shape), pipeline_mode=pl.Buffered(1))


def _projections(x, consts, tables):
    B, S, _ = x.shape
    tm = TM_PROJ
    tok = lambda w: pl.BlockSpec((None, tm, w), lambda b, i: (b, i, 0))
    tab = pl.BlockSpec((tm, LANES), lambda b, i: (i, 0))
    widths = (MLA_HEADS * LANES, MLA_HEADS * LANES, MLA_HEADS * MLA_V,
              GQA_GROUP * LANES, LANES, LANES)
    return pl.pallas_call(
        _proj_kernel,
        grid=(B, S // tm),
        in_specs=[tok(D_MODEL)] + [_const_spec(c.shape) for c in consts] + [tab] * len(tables),
        out_specs=[tok(w) for w in widths],
        out_shape=[jax.ShapeDtypeStruct((B, S, w), jnp.bfloat16) for w in widths],
        compiler_params=pltpu.CompilerParams(
            dimension_semantics=("parallel", "parallel"), vmem_limit_bytes=VMEM_LIMIT_BYTES),
        name="projections",
    )(x, *consts, *tables)


def _attn_kernel(q_ref, k_ref, v_ref, o_ref, *, half):
    q, k, v = q_ref[...], k_ref[...], v_ref[...]
    v_lo = lax.broadcasted_iota(jnp.int32, v.shape, 1) < V_HALF
    zero = jnp.zeros((), v.dtype)
    if half % LANES == 0:
        qs, ks = (q[:, :half], q[:, half:]), (k[:, :half], k[:, half:])
    else:
        k_lo = lax.broadcasted_iota(jnp.int32, k.shape, 1) < half
        qs, ks = (q, q), (jnp.where(k_lo, k, zero), jnp.where(k_lo, zero, k))
    vs = (jnp.where(v_lo, v, zero), jnp.where(v_lo, zero, v))

    acc, inv_l = None, []
    for qh, kh, vh in zip(qs, ks, vs):
        s = lax.dot_general(qh, kh, (((1,), (1,)), ((), ())), preferred_element_type=jnp.float32)
        p = jnp.exp2(s - jnp.max(s, axis=-1, keepdims=True))
        inv_l.append(1.0 / jnp.sum(p, axis=-1, keepdims=True))
        o = jnp.dot(p.astype(v.dtype), vh, preferred_element_type=jnp.float32)
        acc = o if acc is None else acc + o
    o_lo = lax.broadcasted_iota(jnp.int32, acc.shape, 1) < V_HALF
    o_ref[...] = (acc * jnp.where(o_lo, inv_l[0], inv_l[1])).astype(o_ref.dtype)


def _attention(q, k, v, *, half, shared_kv):
    B, S, qw = q.shape
    slot = 2 * half
    tq = TQ_ATTN
    kv_idx = (lambda b, p, i: (b, 0, 0)) if shared_kv else (lambda b, p, i: (b, 0, p))
    return pl.pallas_call(
        functools.partial(_attn_kernel, half=half),
        grid=(B, qw // slot, S // tq),
        in_specs=[pl.BlockSpec((None, tq, slot), lambda b, p, i: (b, i, p)),
                  pl.BlockSpec((None, S, slot), kv_idx),
                  pl.BlockSpec((None, S, LANES), kv_idx)],
        out_specs=pl.BlockSpec((None, tq, LANES), lambda b, p, i: (b, i, p)),
        out_shape=jax.ShapeDtypeStruct((B, S, (qw // slot) * LANES), jnp.bfloat16),
        compiler_params=pltpu.CompilerParams(
            dimension_semantics=("parallel", "parallel", "parallel"),
            vmem_limit_bytes=VMEM_LIMIT_BYTES),
        name="attention_shared_kv" if shared_kv else "attention_per_head_kv",
    )(q, k, v)


def _ffn_kernel(x_ref, oa_ref, ob_ref, goa_ref, gob_ref, woa_ref, wob_ref, g2_ref,
                wg_ref, wu_ref, wd_ref, out_ref):
    oa = oa_ref[...].astype(jnp.float32)
    ob = ob_ref[...].astype(jnp.float32)
    na = oa * _rms_scale(oa, oa.shape[-1]) * goa_ref[...]
    nb = ob * _rms_scale(ob, ob.shape[-1]) * gob_ref[...]
    x1 = x_ref[...] + _bf16_dot(na, woa_ref[...]) + _bf16_dot(nb, wob_ref[...])
    h2 = (x1 * _rms_scale(x1, D_MODEL) * g2_ref[...]).astype(jnp.bfloat16)
    y = x1
    for lo, hi in zip(FFN_BOUNDS[:-1], FFN_BOUNDS[1:]):
        cs = slice(lo, hi)
        g = jnp.dot(h2, wg_ref[:, cs], preferred_element_type=jnp.float32)
        u = jnp.dot(h2, wu_ref[:, cs], preferred_element_type=jnp.float32)
        silu = g / (1.0 + jnp.exp(-g))
        y = y + _bf16_dot(silu * u, wd_ref[cs, :])
    out_ref[...] = y


def _merge_ffn(x, oa, ob, consts):
    B, S, _ = x.shape
    tm = TM_FFN
    tok = lambda w: pl.BlockSpec((None, tm, w), lambda b, i: (b, i, 0))
    return pl.pallas_call(
        _ffn_kernel,
        grid=(B, S // tm),
        in_specs=[tok(D_MODEL), tok(oa.shape[-1]), tok(ob.shape[-1])]
                 + [_const_spec(c.shape) for c in consts],
        out_specs=tok(D_MODEL),
        out_shape=jax.ShapeDtypeStruct(x.shape, x.dtype),
        compiler_params=pltpu.CompilerParams(
            dimension_semantics=("parallel", "parallel"), vmem_limit_bytes=VMEM_LIMIT_BYTES),
        name="merge_ffn",
    )(x, oa, ob, *consts)


def _rope_tables(seq, d_rot, lane_off, reps):
    d_ax = d_rot // 2
    n = d_ax // 2
    inv = ROPE_THETA ** (-(jnp.arange(0, d_ax, 2, dtype=jnp.float32) / d_ax))
    rows = seq // GRID_W
    row = jnp.repeat(jnp.arange(rows, dtype=jnp.float32), GRID_W)
    col = jnp.tile(jnp.arange(GRID_W, dtype=jnp.float32), rows)
    ang_r, ang_c = row[:, None] * inv[None, :], col[:, None] * inv[None, :]
    z = jnp.zeros((seq, n), jnp.float32)
    c = jnp.concatenate([jnp.cos(ang_r)] * 2 + [jnp.cos(ang_c)] * 2, axis=-1)
    sa = jnp.concatenate([-jnp.sin(ang_r), z, -jnp.sin(ang_c), z], axis=-1)
    sb = jnp.concatenate([z, jnp.sin(ang_r), z, jnp.sin(ang_c)], axis=-1)
    sub = LANES // reps
    back = sub - lane_off - d_rot

    def place(t, front):
        t = jnp.concatenate([jnp.full((seq, lane_off), front, jnp.float32), t,
                             jnp.zeros((seq, back), jnp.float32)], axis=-1)
        return jnp.tile(t, (1, reps))

    return place(c, 1.0), place(sa, 0.0), place(sb, 0.0)


def _row(v):
    return v.reshape(1, -1).astype(jnp.float32)


def _pad_lanes(v, front, width):
    return jnp.pad(v, [(0, 0)] * (v.ndim - 1) + [(front, width - front - v.shape[-1])])


def _gqa_slot_order(a, axis):
    shape = a.shape
    a = a.reshape(shape[:axis] + (GQA_KV_HEADS, GQA_GROUP, GQA_HD) + shape[axis + 1:])
    a = jnp.swapaxes(a, axis, axis + 1)
    return a.reshape(shape)


def kernel(x, norm1_g, w_in, q_a_norm_g, w_q_b, kv_a_norm_g, w_kv_b, mla_q_norm_g, mla_k_norm_g,
           gqa_q_norm_g, gqa_k_norm_g, mla_out_norm_g, gqa_out_norm_g, w_o, norm2_g, w_gate, w_up,
           w_down):
    B, S, _ = x.shape
    bf = jnp.bfloat16
    depth = w_in.shape[0]
    assert S % GRID_W == 0 and S % TM_PROJ == 0 and S % TQ_ATTN == 0 and S % TM_FFN == 0
    assert all(b % MXU_DEPTH == 0 for b in FFN_BOUNDS) and FFN_BOUNDS[-1] == D_FF

    tables = (_rope_tables(S, MLA_ROPE, MLA_NOPE, 1) + _rope_tables(S, GQA_HD, 0, 2))

    for l in range(depth):
        wi = w_in[l]
        o0 = Q_LORA + KV_LORA
        o1 = o0 + MLA_ROPE
        o2 = o1 + GQA_HEADS * GQA_HD
        win = jnp.concatenate([
            wi[:, :o0],
            _pad_lanes(wi[:, o0:o1], MLA_NOPE, LANES),
            _gqa_slot_order(wi[:, o1:o2], 1),
            wi[:, o2:]], axis=1).astype(bf)
        wqb = _pad_lanes(w_q_b[l].reshape(Q_LORA, MLA_HEADS, MLA_QK), 0, LANES)
        wqb = wqb.reshape(Q_LORA, MLA_HEADS * LANES).astype(bf)
        wkv = w_kv_b[l].reshape(KV_LORA, MLA_HEADS, MLA_NOPE + MLA_V)
        wkvb = jnp.concatenate([
            _pad_lanes(wkv[..., :MLA_NOPE], 0, LANES).reshape(KV_LORA, MLA_HEADS * LANES),
            wkv[..., MLA_NOPE:].reshape(KV_LORA, MLA_HEADS * MLA_V)], axis=1).astype(bf)
        proj_consts = (
            _row(norm1_g[l]), win, _row(q_a_norm_g[l]), wqb, _row(kv_a_norm_g[l]), wkvb,
            _row(_pad_lanes(mla_q_norm_g[l], 0, LANES)),
            _row(_pad_lanes(mla_k_norm_g[l][:MLA_NOPE], 0, LANES)),
            _row(_pad_lanes(mla_k_norm_g[l][MLA_NOPE:], MLA_NOPE, LANES)),
            _row(jnp.tile(gqa_q_norm_g[l], 2)), _row(jnp.tile(gqa_k_norm_g[l], 2)))
        qa, ka, va, qb, kb, vb = _projections(x, proj_consts, tables)

        oa = _attention(qa, ka, va, half=LANES, shared_kv=False)
        ob = _attention(qb, kb, vb, half=GQA_HD, shared_kv=True)

        n_a = MLA_HEADS * MLA_V
        ffn_consts = (
            _row(mla_out_norm_g[l]), _row(_gqa_slot_order(gqa_out_norm_g[l], 0)),
            w_o[l][:n_a].astype(bf), _gqa_slot_order(w_o[l][n_a:], 0).astype(bf),
            _row(norm2_g[l]), w_gate[l].astype(bf), w_up[l].astype(bf), w_down[l].astype(bf))
        x = _merge_ffn(x, oa, ob, ffn_consts)
    return x
```

```python
import math

import jax
import jax.numpy as jnp
from jax import lax
from jax.experimental import pallas as pl
from jax.experimental.pallas import tpu as pltpu

D_MODEL = 1024
GRID_W = 64
ROPE_THETA = 10000.0
EPS = 1e-6

MLA_HEADS = 8
Q_LORA = 384
KV_LORA = 256
MLA_NOPE = 64
MLA_ROPE = 32
MLA_V = 64
MLA_QK = MLA_NOPE + MLA_ROPE

GQA_HEADS = 8
GQA_KV_HEADS = 2
GQA_HD = 64
GQA_GROUP = GQA_HEADS // GQA_KV_HEADS

D_FF = 2816

LANES = 128
V_HALF = 64
LOG2E = math.log2(math.e)
VMEM_LIMIT_BYTES = 56 * 1024 * 1024

TM_PROJ = 512
TQ_BLOCK = 1024
TQ_ATTN = 256
KV_TILE = 256
TM_FFN = 512

MLA_PAIRS = MLA_HEADS // 2
N_PAIRS = MLA_PAIRS + GQA_GROUP
Q_SLOTS = MLA_HEADS + GQA_GROUP
K_SLOTS = MLA_HEADS + GQA_KV_HEADS
V_SLOTS = MLA_PAIRS + 1

_P_CQ = (0, Q_LORA)
_P_CKV = (_P_CQ[1], _P_CQ[1] + KV_LORA)
_P_KPE = (_P_CKV[1], _P_CKV[1] + LANES)
_P_GQ = (_P_KPE[1], _P_KPE[1] + GQA_GROUP * LANES)
_P_GK = (_P_GQ[1], _P_GQ[1] + GQA_KV_HEADS * LANES)
_P_GV = (_P_GK[1], _P_GK[1] + LANES)
P_WIDTH = _P_GV[1]


def _bf16_dot(a, b):
    return jnp.dot(a.astype(jnp.bfloat16), b, preferred_element_type=jnp.float32)


def _rms_scale(x, width):
    return lax.rsqrt(jnp.sum(x * x, axis=-1, keepdims=True) * (1.0 / width) + EPS)


def _rope(x, c, sa, sb, shift):
    return x * c + pltpu.roll(x, LANES - shift, 1) * sa + pltpu.roll(x, shift, 1) * sb


def _proj_kernel(x_ref, g1_ref, win_ref, gcq_ref, wqb_ref, gckv_ref, wkvb_ref,
                 gmq_ref, gmkn_ref, gmkp_ref, ggq_ref, ggk_ref,
                 cm_ref, sam_ref, sbm_ref, cg_ref, sag_ref, sbg_ref,
                 q_ref, k_ref, v_ref):
    x = x_ref[...]
    h = x * _rms_scale(x, D_MODEL) * g1_ref[...]
    p = _bf16_dot(h, win_ref[...])

    cq = p[:, _P_CQ[0]:_P_CQ[1]]
    ckv = p[:, _P_CKV[0]:_P_CKV[1]]
    kpe = p[:, _P_KPE[0]:_P_KPE[1]]
    q_lat = _bf16_dot(cq * _rms_scale(cq, Q_LORA) * gcq_ref[...], wqb_ref[...])
    kv = _bf16_dot(ckv * _rms_scale(ckv, KV_LORA) * gckv_ref[...], wkvb_ref[...])

    cm, sam, sbm = cm_ref[...], sam_ref[...], sbm_ref[...]
    gmq = gmq_ref[...] * (MLA_QK ** -0.5 * LOG2E)
    gmkn = gmkn_ref[...]
    kpe_rot = _rope(kpe * gmkp_ref[...], cm, sam, sbm, MLA_ROPE // 4)
    ss_pe = jnp.sum(kpe * kpe, axis=-1, keepdims=True)
    for hh in range(MLA_HEADS):
        sl = slice(hh * LANES, (hh + 1) * LANES)
        qh = q_lat[:, sl]
        qn = qh * _rms_scale(qh, MLA_QK) * gmq
        q_ref[hh] = _rope(qn, cm, sam, sbm, MLA_ROPE // 4).astype(q_ref.dtype)
        kn = kv[:, sl]
        r = lax.rsqrt((jnp.sum(kn * kn, axis=-1, keepdims=True) + ss_pe) * (1.0 / MLA_QK) + EPS)
        k_ref[hh] = ((kn * gmkn + kpe_rot) * r).astype(k_ref.dtype)
    for j in range(MLA_PAIRS):
        lo = MLA_HEADS * LANES + j * LANES
        v_ref[j] = kv[:, lo:lo + LANES].astype(v_ref.dtype)

    cg, sag, sbg = cg_ref[...], sag_ref[...], sbg_ref[...]
    low = lax.broadcasted_iota(jnp.int32, (x.shape[0], LANES), 1) < GQA_HD
    ggq = ggq_ref[...] * (GQA_HD ** -0.5 * LOG2E)
    for s in range(GQA_GROUP):
        seg = p[:, _P_GQ[0] + s * LANES:_P_GQ[0] + (s + 1) * LANES]
        sq = seg * seg
        ss_lo = jnp.sum(jnp.where(low, sq, 0.0), axis=-1, keepdims=True)
        ss_hi = jnp.sum(jnp.where(low, 0.0, sq), axis=-1, keepdims=True)
        r = jnp.where(low, lax.rsqrt(ss_lo * (1.0 / GQA_HD) + EPS), lax.rsqrt(ss_hi * (1.0 / GQA_HD) + EPS))
        q_ref[MLA_HEADS + s] = _rope(seg * r * ggq, cg, sag, sbg, GQA_HD // 4).astype(q_ref.dtype)
    ggk = ggk_ref[...]
    for j in range(GQA_KV_HEADS):
        seg = p[:, _P_GK[0] + j * LANES:_P_GK[0] + (j + 1) * LANES]
        kn = seg * _rms_scale(seg, GQA_HD) * ggk
        k_ref[MLA_HEADS + j] = _rope(kn, cg, sag, sbg, GQA_HD // 4).astype(k_ref.dtype)
    v_ref[MLA_PAIRS] = p[:, _P_GV[0]:_P_GV[1]].astype(v_ref.dtype)


def _const_spec(shape):
    return pl.BlockSpec(shape, lambda *_: (0,) * len(shape), pipeline_mode=pl.Buffered(1))


def _projections(x, consts, tables):
    B, S, _ = x.shape
    tm = TM_PROJ
    slots = lambda n: pl.BlockSpec((None, n, tm, LANES), lambda b, i: (b, 0, i, 0))
    tab = pl.BlockSpec((tm, LANES), lambda b, i: (i, 0))
    n_slots = (Q_SLOTS, K_SLOTS, V_SLOTS)
    return pl.pallas_call(
        _proj_kernel,
        grid=(B, S // tm),
        in_specs=[pl.BlockSpec((None, tm, D_MODEL), lambda b, i: (b, i, 0))]
                 + [_const_spec(c.shape) for c in consts] + [tab] * len(tables),
        out_specs=[slots(n) for n in n_slots],
        out_shape=[jax.ShapeDtypeStruct((B, n, S, LANES), jnp.bfloat16) for n in n_slots],
        compiler_params=pltpu.CompilerParams(
            dimension_semantics=("parallel", "parallel"), vmem_limit_bytes=VMEM_LIMIT_BYTES),
        name="projections",
    )(x, *consts, *tables)


def _attn_kernel(q_ref, k_ref, v_ref, o_ref, sc0, sc1, mx0, mx1):
    tq, kv_len = sc0.shape[1:]
    n_items = (q_ref.shape[1] // tq) * N_PAIRS
    sc, mx = (sc0, sc1), (mx0, mx1)
    v_low = lax.broadcasted_iota(jnp.int32, (KV_TILE, LANES), 1) < V_HALF
    o_low = lax.broadcasted_iota(jnp.int32, (tq, LANES), 1) < V_HALF
    zero = jnp.zeros((), v_ref.dtype)

    def item(it):
        it = jnp.minimum(it, n_items - 1)
        return pl.multiple_of((it // N_PAIRS) * tq, tq), it % N_PAIRS

    def stage_scores(it, s_dst, m_dst):
        row0, j = item(it)
        mla = j < MLA_PAIRS
        for e in range(2):
            qi = jnp.where(mla, 2 * j + e, MLA_HEADS + j - MLA_PAIRS)
            ki = jnp.where(mla, 2 * j + e, MLA_HEADS + e)
            s = lax.dot_general(q_ref[qi, pl.ds(row0, tq), :], k_ref[ki],
                                (((1,), (1,)), ((), ())), preferred_element_type=jnp.float32)
            s_dst[e] = s
            m_dst[e] = jnp.broadcast_to(jnp.max(s, axis=-1, keepdims=True), (tq, LANES))

    def stage_values(it, s_src, m_src):
        row0, j = item(it)
        vi = jnp.minimum(j, MLA_PAIRS)
        acc, inv_l = None, []
        for e in range(2):
            m = m_src[e]
            l_part = None
            for t in range(kv_len // KV_TILE):
                cols = []
                for c in range(t * KV_TILE, (t + 1) * KV_TILE, LANES):
                    p = jnp.exp2(s_src[e, :, c:c + LANES] - m)
                    l_part = p if l_part is None else l_part + p
                    cols.append(p.astype(v_ref.dtype))
                v = v_ref[vi, t * KV_TILE:(t + 1) * KV_TILE, :]
                vh = jnp.where(v_low, v, zero) if e == 0 else jnp.where(v_low, zero, v)
                o = jnp.dot(jnp.concatenate(cols, axis=-1), vh, preferred_element_type=jnp.float32)
                acc = o if acc is None else acc + o
            inv_l.append(1.0 / jnp.sum(l_part, axis=-1, keepdims=True))
        o_ref[j, pl.ds(row0, tq), :] = (acc * jnp.where(o_low, inv_l[0], inv_l[1])).astype(o_ref.dtype)

    stage_scores(0, sc0, mx0)

    def trip(it, carry):
        for cur in range(2):
            @pl.when(it % 2 == cur)
            def _():
                stage_scores(it + 1, sc[1 - cur], mx[1 - cur])
                stage_values(it, sc[cur], mx[cur])
        return carry

    lax.fori_loop(0, n_items, trip, 0)


def _attention(q, k, v):
    B, _, S, _ = q.shape
    tb, tq = TQ_BLOCK, TQ_ATTN
    pair_buf = lambda w: pltpu.VMEM((2, tq, w), jnp.float32)
    return pl.pallas_call(
        _attn_kernel,
        grid=(B, S // tb),
        in_specs=[pl.BlockSpec((None, Q_SLOTS, tb, LANES), lambda b, i: (b, 0, i, 0)),
                  pl.BlockSpec((None, K_SLOTS, S, LANES), lambda b, i: (b, 0, 0, 0)),
                  pl.BlockSpec((None, V_SLOTS, S, LANES), lambda b, i: (b, 0, 0, 0))],
        out_specs=pl.BlockSpec((None, N_PAIRS, tb, LANES), lambda b, i: (b, 0, i, 0)),
        out_shape=jax.ShapeDtypeStruct((B, N_PAIRS, S, LANES), jnp.bfloat16),
        scratch_shapes=[pair_buf(S), pair_buf(S), pair_buf(LANES), pair_buf(LANES)],
        compiler_params=pltpu.CompilerParams(
            dimension_semantics=("parallel", "parallel"), vmem_limit_bytes=VMEM_LIMIT_BYTES),
        name="attention",
    )(q, k, v)


def _ffn_kernel(x_ref, o_ref, goa_ref, gob_ref, wo_ref, g2_ref, wg_ref, wu_ref, wd_ref, out_ref):
    oa = jnp.concatenate([o_ref[j] for j in range(MLA_PAIRS)], axis=-1).astype(jnp.float32)
    ob = jnp.concatenate([o_ref[j] for j in range(MLA_PAIRS, N_PAIRS)], axis=-1).astype(jnp.float32)
    na = oa * _rms_scale(oa, oa.shape[-1]) * goa_ref[...]
    nb = ob * _rms_scale(ob, ob.shape[-1]) * gob_ref[...]
    mixed = jnp.concatenate([na, nb], axis=-1)
    x1 = x_ref[...] + _bf16_dot(mixed, wo_ref[...])
    h2 = (x1 * _rms_scale(x1, D_MODEL) * g2_ref[...]).astype(jnp.bfloat16)
    g = jnp.dot(h2, wg_ref[...], preferred_element_type=jnp.float32)
    u = jnp.dot(h2, wu_ref[...], preferred_element_type=jnp.float32)
    silu = g / (1.0 + jnp.exp(-g))
    out_ref[...] = x1 + _bf16_dot(silu * u, wd_ref[...])


def _merge_ffn(x, o, consts):
    B, S, _ = x.shape
    tm = TM_FFN
    tok = pl.BlockSpec((None, tm, D_MODEL), lambda b, i: (b, i, 0))
    return pl.pallas_call(
        _ffn_kernel,
        grid=(B, S // tm),
        in_specs=[tok, pl.BlockSpec((None, N_PAIRS, tm, LANES), lambda b, i: (b, 0, i, 0))]
                 + [_const_spec(c.shape) for c in consts],
        out_specs=tok,
        out_shape=jax.ShapeDtypeStruct(x.shape, x.dtype),
        compiler_params=pltpu.CompilerParams(
            dimension_semantics=("parallel", "parallel"), vmem_limit_bytes=VMEM_LIMIT_BYTES),
        name="merge_ffn",
    )(x, o, *consts)


def _rope_tables(seq, d_rot, lane_off, reps):
    d_ax = d_rot // 2
    n = d_ax // 2
    inv = ROPE_THETA ** (-(jnp.arange(0, d_ax, 2, dtype=jnp.float32) / d_ax))
    rows = seq // GRID_W
    row = jnp.repeat(jnp.arange(rows, dtype=jnp.float32), GRID_W)
    col = jnp.tile(jnp.arange(GRID_W, dtype=jnp.float32), rows)
    ang_r, ang_c = row[:, None] * inv[None, :], col[:, None] * inv[None, :]
    z = jnp.zeros((seq, n), jnp.float32)
    c = jnp.concatenate([jnp.cos(ang_r)] * 2 + [jnp.cos(ang_c)] * 2, axis=-1)
    sa = jnp.concatenate([-jnp.sin(ang_r), z, -jnp.sin(ang_c), z], axis=-1)
    sb = jnp.concatenate([z, jnp.sin(ang_r), z, jnp.sin(ang_c)], axis=-1)
    sub = LANES // reps
    back = sub - lane_off - d_rot

    def place(t, front):
        t = jnp.concatenate([jnp.full((seq, lane_off), front, jnp.float32), t,
                             jnp.zeros((seq, back), jnp.float32)], axis=-1)
        return jnp.tile(t, (1, reps))

    return place(c, 1.0), place(sa, 0.0), place(sb, 0.0)


def _row(v):
    return v.reshape(1, -1).astype(jnp.float32)


def _pad_lanes(v, front, width):
    return jnp.pad(v, [(0, 0)] * (v.ndim - 1) + [(front, width - front - v.shape[-1])])


def _gqa_slot_order(a, axis):
    shape = a.shape
    a = a.reshape(shape[:axis] + (GQA_KV_HEADS, GQA_GROUP, GQA_HD) + shape[axis + 1:])
    a = jnp.swapaxes(a, axis, axis + 1)
    return a.reshape(shape)


def kernel(x, norm1_g, w_in, q_a_norm_g, w_q_b, kv_a_norm_g, w_kv_b, mla_q_norm_g, mla_k_norm_g,
           gqa_q_norm_g, gqa_k_norm_g, mla_out_norm_g, gqa_out_norm_g, w_o, norm2_g, w_gate, w_up,
           w_down):
    B, S, _ = x.shape
    bf = jnp.bfloat16
    depth = w_in.shape[0]
    assert S % GRID_W == 0 and S % TM_PROJ == 0 and S % TM_FFN == 0
    assert S % TQ_BLOCK == 0 and TQ_BLOCK % TQ_ATTN == 0

    tables = (_rope_tables(S, MLA_ROPE, MLA_NOPE, 1) + _rope_tables(S, GQA_HD, 0, 2))

    for l in range(depth):
        wi = w_in[l]
        o0 = Q_LORA + KV_LORA
        o1 = o0 + MLA_ROPE
        o2 = o1 + GQA_HEADS * GQA_HD
        o3 = o2 + GQA_KV_HEADS * GQA_HD
        win = jnp.concatenate([
            wi[:, :o0],
            _pad_lanes(wi[:, o0:o1], MLA_NOPE, LANES),
            _gqa_slot_order(wi[:, o1:o2], 1),
            _pad_lanes(wi[:, o2:o2 + GQA_HD], 0, LANES),
            _pad_lanes(wi[:, o2 + GQA_HD:o3], GQA_HD, LANES),
            wi[:, o3:]], axis=1).astype(bf)
        wqb = _pad_lanes(w_q_b[l].reshape(Q_LORA, MLA_HEADS, MLA_QK), 0, LANES)
        wqb = wqb.reshape(Q_LORA, MLA_HEADS * LANES).astype(bf)
        wkv = w_kv_b[l].reshape(KV_LORA, MLA_HEADS, MLA_NOPE + MLA_V)
        wkvb = jnp.concatenate([
            _pad_lanes(wkv[..., :MLA_NOPE], 0, LANES).reshape(KV_LORA, MLA_HEADS * LANES),
            wkv[..., MLA_NOPE:].reshape(KV_LORA, MLA_HEADS * MLA_V)], axis=1).astype(bf)
        proj_consts = (
            _row(norm1_g[l]), win, _row(q_a_norm_g[l]), wqb, _row(kv_a_norm_g[l]), wkvb,
            _row(_pad_lanes(mla_q_norm_g[l], 0, LANES)),
            _row(_pad_lanes(mla_k_norm_g[l][:MLA_NOPE], 0, LANES)),
            _row(_pad_lanes(mla_k_norm_g[l][MLA_NOPE:], MLA_NOPE, LANES)),
            _row(jnp.tile(gqa_q_norm_g[l], 2)), _row(jnp.tile(gqa_k_norm_g[l], 2)))
        q, k, v = _projections(x, proj_consts, tables)

        o = _attention(q, k, v)

        n_a = MLA_HEADS * MLA_V
        wo = jnp.concatenate([w_o[l][:n_a], _gqa_slot_order(w_o[l][n_a:], 0)], axis=0).astype(bf)
        ffn_consts = (
            _row(mla_out_norm_g[l]), _row(_gqa_slot_order(gqa_out_norm_g[l], 0)), wo,
            _row(norm2_g[l]), w_gate[l].astype(bf), w_up[l].astype(bf), w_down[l].astype(bf))
        x = _merge_ffn(x, o, ffn_consts)
    return x
```

```python
import math

import jax
import jax.numpy as jnp
from jax import lax
from jax.experimental import pallas as pl
from jax.experimental.pallas import tpu as pltpu

D_MODEL = 1024
GRID_W = 64
ROPE_THETA = 10000.0
EPS = 1e-6

MLA_HEADS = 8
Q_LORA = 384
KV_LORA = 256
MLA_NOPE = 64
MLA_ROPE = 32
MLA_V = 64
MLA_QK = MLA_NOPE + MLA_ROPE

GQA_HEADS = 8
GQA_KV_HEADS = 2
GQA_HD = 64
GQA_GROUP = GQA_HEADS // GQA_KV_HEADS

D_FF = 2816

LANES = 128
V_HALF = 64
LOG2E = math.log2(math.e)
VMEM_LIMIT_BYTES = 56 * 1024 * 1024

TM_PROJ = 512
TQ_BLOCK = 1024
TQ_ATTN = 512
KV_TILE = 256
TM_FFN = 512

MLA_PAIRS = MLA_HEADS // 2
N_PAIRS = MLA_PAIRS + GQA_GROUP
Q_SLOTS = MLA_HEADS + GQA_GROUP
K_SLOTS = MLA_HEADS + GQA_KV_HEADS
V_SLOTS = MLA_PAIRS + 1

_P_CQ = (0, Q_LORA)
_P_CKV = (_P_CQ[1], _P_CQ[1] + KV_LORA)
_P_KPE = (_P_CKV[1], _P_CKV[1] + LANES)
_P_GQ = (_P_KPE[1], _P_KPE[1] + GQA_GROUP * LANES)
_P_GK = (_P_GQ[1], _P_GQ[1] + GQA_KV_HEADS * LANES)
_P_GV = (_P_GK[1], _P_GK[1] + LANES)
P_WIDTH = _P_GV[1]


def _bf16_dot(a, b):
    return jnp.dot(a.astype(jnp.bfloat16), b, preferred_element_type=jnp.float32)


def _rms_scale(x, width):
    return lax.rsqrt(jnp.sum(x * x, axis=-1, keepdims=True) * (1.0 / width) + EPS)


def _rope(x, c, sa, sb, shift):
    return x * c + pltpu.roll(x, LANES - shift, 1) * sa + pltpu.roll(x, shift, 1) * sb


def _proj_kernel(x_ref, g1_ref, win_ref, gcq_ref, wqb_ref, gckv_ref, wkvb_ref,
                 gmq_ref, gmkn_ref, gmkp_ref, ggq_ref, ggk_ref,
                 cm_ref, sam_ref, sbm_ref, cg_ref, sag_ref, sbg_ref,
                 q_ref, k_ref, v_ref):
    x = x_ref[...]
    h = x * _rms_scale(x, D_MODEL) * g1_ref[...]
    p = _bf16_dot(h, win_ref[...])

    cq = p[:, _P_CQ[0]:_P_CQ[1]]
    ckv = p[:, _P_CKV[0]:_P_CKV[1]]
    kpe = p[:, _P_KPE[0]:_P_KPE[1]]
    q_lat = _bf16_dot(cq * _rms_scale(cq, Q_LORA) * gcq_ref[...], wqb_ref[...])
    kv = _bf16_dot(ckv * _rms_scale(ckv, KV_LORA) * gckv_ref[...], wkvb_ref[...])

    cm, sam, sbm = cm_ref[...], sam_ref[...], sbm_ref[...]
    gmq = gmq_ref[...] * (MLA_QK ** -0.5 * LOG2E)
    gmkn = gmkn_ref[...]
    kpe_rot = _rope(kpe * gmkp_ref[...], cm, sam, sbm, MLA_ROPE // 4)
    ss_pe = jnp.sum(kpe * kpe, axis=-1, keepdims=True)
    for hh in range(MLA_HEADS):
        sl = slice(hh * LANES, (hh + 1) * LANES)
        qh = q_lat[:, sl]
        qn = qh * _rms_scale(qh, MLA_QK) * gmq
        q_ref[hh] = _rope(qn, cm, sam, sbm, MLA_ROPE // 4).astype(q_ref.dtype)
        kn = kv[:, sl]
        r = lax.rsqrt((jnp.sum(kn * kn, axis=-1, keepdims=True) + ss_pe) * (1.0 / MLA_QK) + EPS)
        k_ref[hh] = ((kn * gmkn + kpe_rot) * r).astype(k_ref.dtype)
    for j in range(MLA_PAIRS):
        lo = MLA_HEADS * LANES + j * LANES
        v_ref[j] = kv[:, lo:lo + LANES].astype(v_ref.dtype)

    cg, sag, sbg = cg_ref[...], sag_ref[...], sbg_ref[...]
    low = lax.broadcasted_iota(jnp.int32, (x.shape[0], LANES), 1) < GQA_HD
    ggq = ggq_ref[...] * (GQA_HD ** -0.5 * LOG2E)
    for s in range(GQA_GROUP):
        seg = p[:, _P_GQ[0] + s * LANES:_P_GQ[0] + (s + 1) * LANES]
        sq = seg * seg
        ss_lo = jnp.sum(jnp.where(low, sq, 0.0), axis=-1, keepdims=True)
        ss_hi = jnp.sum(jnp.where(low, 0.0, sq), axis=-1, keepdims=True)
        r = jnp.where(low, lax.rsqrt(ss_lo * (1.0 / GQA_HD) + EPS), lax.rsqrt(ss_hi * (1.0 / GQA_HD) + EPS))
        q_ref[MLA_HEADS + s] = _rope(seg * r * ggq, cg, sag, sbg, GQA_HD // 4).astype(q_ref.dtype)
    ggk = ggk_ref[...]
    for j in range(GQA_KV_HEADS):
        seg = p[:, _P_GK[0] + j * LANES:_P_GK[0] + (j + 1) * LANES]
        kn = seg * _rms_scale(seg, GQA_HD) * ggk
        k_ref[MLA_HEADS + j] = _rope(kn, cg, sag, sbg, GQA_HD // 4).astype(k_ref.dtype)
    v_ref[MLA_PAIRS] = p[:, _P_GV[0]:_P_GV[1]].astype(v_ref.dtype)


def _const_spec(shape):
    return pl.BlockSpec(shape, lambda *_: (0,) * len(shape), pipeline_mode=pl.Buffered(1))


def _projections(x, consts, tables):
    B, S, _ = x.shape
    tm = TM_PROJ
    slots = lambda n: pl.BlockSpec((None, n, tm, LANES), lambda b, i: (b, 0, i, 0))
    tab = pl.BlockSpec((tm, LANES), lambda b, i: (i, 0))
    n_slots = (Q_SLOTS, K_SLOTS, V_SLOTS)
    return pl.pallas_call(
        _proj_kernel,
        grid=(B, S // tm),
        in_specs=[pl.BlockSpec((None, tm, D_MODEL), lambda b, i: (b, i, 0))]
                 + [_const_spec(c.shape) for c in consts] + [tab] * len(tables),
        out_specs=[slots(n) for n in n_slots],
        out_shape=[jax.ShapeDtypeStruct((B, n, S, LANES), jnp.bfloat16) for n in n_slots],
        compiler_params=pltpu.CompilerParams(
            dimension_semantics=("parallel", "parallel"), vmem_limit_bytes=VMEM_LIMIT_BYTES),
        name="projections",
    )(x, *consts, *tables)


def _attn_kernel(q_ref, k_ref, v_ref, o_ref, sc0, sc1, mx0, mx1):
    tq, kv_len = sc0.shape[1:]
    n_items = (q_ref.shape[1] // tq) * N_PAIRS
    sc, mx = (sc0, sc1), (mx0, mx1)
    v_low = lax.broadcasted_iota(jnp.int32, (kv_len, LANES), 1) < V_HALF
    o_low = lax.broadcasted_iota(jnp.int32, (tq, LANES), 1) < V_HALF
    one = jnp.ones((), v_ref.dtype)

    def item(it):
        it = jnp.minimum(it, n_items - 1)
        return pl.multiple_of((it // N_PAIRS) * tq, tq), it % N_PAIRS

    def stage_scores(it, s_dst, m_dst):
        row0, j = item(it)
        mla = j < MLA_PAIRS
        for e in range(2):
            qi = jnp.where(mla, 2 * j + e, MLA_HEADS + j - MLA_PAIRS)
            ki = jnp.where(mla, 2 * j + e, MLA_HEADS + e)
            s = lax.dot_general(q_ref[qi, pl.ds(row0, tq), :], k_ref[ki],
                                (((1,), (1,)), ((), ())), preferred_element_type=jnp.float32)
            s_dst[e] = s
            m_dst[e] = jnp.broadcast_to(jnp.max(s, axis=-1, keepdims=True), (tq, LANES))

    def stage_values(it, s_src, m_src):
        row0, j = item(it)
        vi = jnp.minimum(j, MLA_PAIRS)
        v = v_ref[vi]
        accs = []
        for e in range(2):
            m = m_src[e]
            p = jnp.concatenate([jnp.exp2(s_src[e, :, c:c + LANES] - m).astype(v.dtype)
                                 for c in range(0, kv_len, LANES)], axis=-1)
            vh = jnp.where(v_low, v, one) if e == 0 else jnp.where(v_low, one, v)
            accs.append(jnp.dot(p, vh, preferred_element_type=jnp.float32))
        num = jnp.where(o_low, accs[0], accs[1])
        den = jnp.where(o_low, pltpu.roll(accs[0], V_HALF, 1), pltpu.roll(accs[1], V_HALF, 1))
        o_ref[j, pl.ds(row0, tq), :] = (num / den).astype(o_ref.dtype)

    stage_scores(0, sc0, mx0)

    def trip(it, carry):
        for cur in range(2):
            @pl.when(it % 2 == cur)
            def _():
                stage_scores(it + 1, sc[1 - cur], mx[1 - cur])
                stage_values(it, sc[cur], mx[cur])
        return carry

    lax.fori_loop(0, n_items, trip, 0)


def _attention(q, k, v):
    B, _, S, _ = q.shape
    tb, tq = TQ_BLOCK, TQ_ATTN
    pair_buf = lambda w: pltpu.VMEM((2, tq, w), jnp.float32)
    return pl.pallas_call(
        _attn_kernel,
        grid=(B, S // tb),
        in_specs=[pl.BlockSpec((None, Q_SLOTS, tb, LANES), lambda b, i: (b, 0, i, 0)),
                  pl.BlockSpec((None, K_SLOTS, S, LANES), lambda b, i: (b, 0, 0, 0)),
                  pl.BlockSpec((None, V_SLOTS, S, LANES), lambda b, i: (b, 0, 0, 0))],
        out_specs=pl.BlockSpec((None, N_PAIRS, tb, LANES), lambda b, i: (b, 0, i, 0)),
        out_shape=jax.ShapeDtypeStruct((B, N_PAIRS, S, LANES), jnp.bfloat16),
        scratch_shapes=[pair_buf(S), pair_buf(S), pair_buf(LANES), pair_buf(LANES)],
        compiler_params=pltpu.CompilerParams(
            dimension_semantics=("parallel", "parallel"), vmem_limit_bytes=VMEM_LIMIT_BYTES),
        name="attention",
    )(q, k, v)


def _ffn_kernel(x_ref, o_ref, goa_ref, gob_ref, wo_ref, g2_ref, wg_ref, wu_ref, wd_ref, out_ref):
    oa = jnp.concatenate([o_ref[j] for j in range(MLA_PAIRS)], axis=-1).astype(jnp.float32)
    ob = jnp.concatenate([o_ref[j] for j in range(MLA_PAIRS, N_PAIRS)], axis=-1).astype(jnp.float32)
    na = oa * _rms_scale(oa, oa.shape[-1]) * goa_ref[...]
    nb = ob * _rms_scale(ob, ob.shape[-1]) * gob_ref[...]
    mixed = jnp.concatenate([na, nb], axis=-1)
    x1 = x_ref[...] + _bf16_dot(mixed, wo_ref[...])
    h2 = (x1 * _rms_scale(x1, D_MODEL) * g2_ref[...]).astype(jnp.bfloat16)
    g = jnp.dot(h2, wg_ref[...], preferred_element_type=jnp.float32)
    u = jnp.dot(h2, wu_ref[...], preferred_element_type=jnp.float32)
    silu = g / (1.0 + jnp.exp(-g))
    out_ref[...] = x1 + _bf16_dot(silu * u, wd_ref[...])


def _merge_ffn(x, o, consts):
    B, S, _ = x.shape
    tm = TM_FFN
    tok = pl.BlockSpec((None, tm, D_MODEL), lambda b, i: (b, i, 0))
    return pl.pallas_call(
        _ffn_kernel,
        grid=(B, S // tm),
        in_specs=[tok, pl.BlockSpec((None, N_PAIRS, tm, LANES), lambda b, i: (b, 0, i, 0))]
                 + [_const_spec(c.shape) for c in consts],
        out_specs=tok,
        out_shape=jax.ShapeDtypeStruct(x.shape, x.dtype),
        compiler_params=pltpu.CompilerParams(
            dimension_semantics=("parallel", "parallel"), vmem_limit_bytes=VMEM_LIMIT_BYTES),
        name="merge_ffn",
    )(x, o, *consts)


def _rope_tables(seq, d_rot, lane_off, reps):
    d_ax = d_rot // 2
    n = d_ax // 2
    inv = ROPE_THETA ** (-(jnp.arange(0, d_ax, 2, dtype=jnp.float32) / d_ax))
    rows = seq // GRID_W
    row = jnp.repeat(jnp.arange(rows, dtype=jnp.float32), GRID_W)
    col = jnp.tile(jnp.arange(GRID_W, dtype=jnp.float32), rows)
    ang_r, ang_c = row[:, None] * inv[None, :], col[:, None] * inv[None, :]
    z = jnp.zeros((seq, n), jnp.float32)
    c = jnp.concatenate([jnp.cos(ang_r)] * 2 + [jnp.cos(ang_c)] * 2, axis=-1)
    sa = jnp.concatenate([-jnp.sin(ang_r), z, -jnp.sin(ang_c), z], axis=-1)
    sb = jnp.concatenate([z, jnp.sin(ang_r), z, jnp.sin(ang_c)], axis=-1)
    sub = LANES // reps
    back = sub - lane_off - d_rot

    def place(t, front):
        t = jnp.concatenate([jnp.full((seq, lane_off), front, jnp.float32), t,
                             jnp.zeros((seq, back), jnp.float32)], axis=-1)
        return jnp.tile(t, (1, reps))

    return place(c, 1.0), place(sa, 0.0), place(sb, 0.0)


def _row(v):
    return v.reshape(1, -1).astype(jnp.float32)


def _pad_lanes(v, front, width):
    return jnp.pad(v, [(0, 0)] * (v.ndim - 1) + [(front, width - front - v.shape[-1])])


def _gqa_slot_order(a, axis):
    shape = a.shape
    a = a.reshape(shape[:axis] + (GQA_KV_HEADS, GQA_GROUP, GQA_HD) + shape[axis + 1:])
    a = jnp.swapaxes(a, axis, axis + 1)
    return a.reshape(shape)


def kernel(x, norm1_g, w_in, q_a_norm_g, w_q_b, kv_a_norm_g, w_kv_b, mla_q_norm_g, mla_k_norm_g,
           gqa_q_norm_g, gqa_k_norm_g, mla_out_norm_g, gqa_out_norm_g, w_o, norm2_g, w_gate, w_up,
           w_down):
    B, S, _ = x.shape
    bf = jnp.bfloat16
    depth = w_in.shape[0]
    assert S % GRID_W == 0 and S % TM_PROJ == 0 and S % TM_FFN == 0
    assert S % TQ_BLOCK == 0 and TQ_BLOCK % TQ_ATTN == 0

    tables = (_rope_tables(S, MLA_ROPE, MLA_NOPE, 1) + _rope_tables(S, GQA_HD, 0, 2))

    for l in range(depth):
        wi = w_in[l]
        o0 = Q_LORA + KV_LORA
        o1 = o0 + MLA_ROPE
        o2 = o1 + GQA_HEADS * GQA_HD
        o3 = o2 + GQA_KV_HEADS * GQA_HD
        win = jnp.concatenate([
            wi[:, :o0],
            _pad_lanes(wi[:, o0:o1], MLA_NOPE, LANES),
            _gqa_slot_order(wi[:, o1:o2], 1),
            _pad_lanes(wi[:, o2:o2 + GQA_HD], 0, LANES),
            _pad_lanes(wi[:, o2 + GQA_HD:o3], GQA_HD, LANES),
            wi[:, o3:]], axis=1).astype(bf)
        wqb = _pad_lanes(w_q_b[l].reshape(Q_LORA, MLA_HEADS, MLA_QK), 0, LANES)
        wqb = wqb.reshape(Q_LORA, MLA_HEADS * LANES).astype(bf)
        wkv = w_kv_b[l].reshape(KV_LORA, MLA_HEADS, MLA_NOPE + MLA_V)
        wkvb = jnp.concatenate([
            _pad_lanes(wkv[..., :MLA_NOPE], 0, LANES).reshape(KV_LORA, MLA_HEADS * LANES),
            wkv[..., MLA_NOPE:].reshape(KV_LORA, MLA_HEADS * MLA_V)], axis=1).astype(bf)
        proj_consts = (
            _row(norm1_g[l]), win, _row(q_a_norm_g[l]), wqb, _row(kv_a_norm_g[l]), wkvb,
            _row(_pad_lanes(mla_q_norm_g[l], 0, LANES)),
            _row(_pad_lanes(mla_k_norm_g[l][:MLA_NOPE], 0, LANES)),
            _row(_pad_lanes(mla_k_norm_g[l][MLA_NOPE:], MLA_NOPE, LANES)),
            _row(jnp.tile(gqa_q_norm_g[l], 2)), _row(jnp.tile(gqa_k_norm_g[l], 2)))
        q, k, v = _projections(x, proj_consts, tables)

        o = _attention(q, k, v)

        n_a = MLA_HEADS * MLA_V
        wo = jnp.concatenate([w_o[l][:n_a], _gqa_slot_order(w_o[l][n_a:], 0)], axis=0).astype(bf)
        ffn_consts = (
            _row(mla_out_norm_g[l]), _row(_gqa_slot_order(gqa_out_norm_g[l], 0)), wo,
            _row(norm2_g[l]), w_gate[l].astype(bf), w_up[l].astype(bf), w_down[l].astype(bf))
        x = _merge_ffn(x, o, ffn_consts)
    return x
```

```python
import math

import jax
import jax.numpy as jnp
from jax import lax
from jax.experimental import pallas as pl
from jax.experimental.pallas import tpu as pltpu

D_MODEL = 1024
GRID_W = 64
ROPE_THETA = 10000.0
EPS = 1e-6

MLA_HEADS = 8
Q_LORA = 384
KV_LORA = 256
MLA_NOPE = 64
MLA_ROPE = 32
MLA_V = 64
MLA_QK = MLA_NOPE + MLA_ROPE

GQA_HEADS = 8
GQA_KV_HEADS = 2
GQA_HD = 64
GQA_GROUP = GQA_HEADS // GQA_KV_HEADS

D_FF = 2816

LANES = 128
V_HALF = 64
LOG2E = math.log2(math.e)
VMEM_LIMIT_BYTES = 56 * 1024 * 1024

TB_PROJ = 1024
TS_PROJ = 256
TQ_BLOCK = 1024
TQ_ATTN = 512
TM_FFN = 512

MLA_PAIRS = MLA_HEADS // 2
N_PAIRS = MLA_PAIRS + GQA_GROUP
Q_SLOTS = MLA_HEADS + GQA_HEADS
K_SLOTS = MLA_HEADS + GQA_KV_HEADS
V_SLOTS = MLA_PAIRS + 1

_P_CQ = (0, Q_LORA)
_P_CKV = (_P_CQ[1], _P_CQ[1] + KV_LORA)
_P_KPE = (_P_CKV[1], _P_CKV[1] + LANES)
_P_GQ = (_P_KPE[1], _P_KPE[1] + GQA_HEADS * LANES)
_P_GK = (_P_GQ[1], _P_GQ[1] + LANES)
_P_GK_P = (_P_GK[1], _P_GK[1] + LANES)
_P_GV = (_P_GK_P[1], _P_GK_P[1] + LANES)
P_WIDTH = _P_GV[1]


def _bf16_dot(a, b):
    return jnp.dot(a.astype(jnp.bfloat16), b, preferred_element_type=jnp.float32)


def _rms_scale(x, width):
    return lax.rsqrt(jnp.sum(x * x, axis=-1, keepdims=True) * (1.0 / width) + EPS)


def _proj_kernel(x_ref, g1_ref, win_ref, gcq_ref, wqb_ref, gckv_ref, wkvb_ref,
                 gmq_ref, gmkn_ref, gmkp_ref, ggq_ref, ggk_ref, ggkp_ref,
                 tmq_ref, tmk_ref, tgq_ref, tgc_ref, tgs_ref,
                 q_ref, k_ref, v_ref, ql0, ql1, kv0, kv1, pr0, pr1):
    ts = ql0.shape[0]
    n_items = x_ref.shape[0] // ts
    ql, kvs, pr = (ql0, ql1), (kv0, kv1), (pr0, pr1)

    def rows(it):
        return pl.ds(pl.multiple_of(jnp.minimum(it, n_items - 1) * ts, ts), ts)

    def stage_matmuls(it, ql_dst, kv_dst, pr_dst):
        x = x_ref[rows(it), :]
        h = x * _rms_scale(x, D_MODEL) * g1_ref[...]
        p = _bf16_dot(h, win_ref[...])
        cq = p[:, _P_CQ[0]:_P_CQ[1]]
        ckv = p[:, _P_CKV[0]:_P_CKV[1]]
        ql_dst[...] = _bf16_dot(cq * _rms_scale(cq, Q_LORA) * gcq_ref[...], wqb_ref[...])
        kv_dst[...] = _bf16_dot(ckv * _rms_scale(ckv, KV_LORA) * gckv_ref[...], wkvb_ref[...])
        pr_dst[...] = p[:, _P_KPE[0]:]

    def stage_heads(it, ql_src, kv_src, pr_src):
        r = rows(it)
        off = lambda span: slice(span[0] - _P_KPE[0], span[1] - _P_KPE[0])
        lane = lax.broadcasted_iota(jnp.int32, (ts, LANES), 1)
        low = lane < LANES // 2

        y = pr_src[:, off(_P_KPE)] * gmkp_ref[...] * tmk_ref[r, :]
        kpe_rot = y + pltpu.roll(y, MLA_ROPE, 1)
        kpe = jnp.where(lane < MLA_ROPE, pr_src[:, off(_P_KPE)], 0.0)
        ss_pe = jnp.sum(kpe * kpe, axis=-1, keepdims=True)
        tmq = tmq_ref[r, :] * (gmq_ref[...] * (MLA_QK ** -0.5 * LOG2E))
        gmkn = gmkn_ref[...]
        for hh in range(MLA_HEADS):
            sl = slice(hh * LANES, (hh + 1) * LANES)
            qh = ql_src[:, sl]
            qsq = jnp.where(lane < MLA_QK, qh * qh, 0.0)
            rq = lax.rsqrt(jnp.sum(qsq, axis=-1, keepdims=True) * (1.0 / MLA_QK) + EPS)
            q_ref[hh, r, :] = (qh * rq * tmq).astype(q_ref.dtype)
            kn = kv_src[:, sl]
            rk = lax.rsqrt((jnp.sum(kn * kn, axis=-1, keepdims=True) + ss_pe) * (1.0 / MLA_QK) + EPS)
            k_ref[hh, r, :] = (jnp.where(low, kn * gmkn, kpe_rot) * rk).astype(k_ref.dtype)
        for j in range(MLA_PAIRS):
            lo = MLA_HEADS * LANES + j * LANES
            v_ref[j, r, :] = kv_src[:, lo:lo + LANES].astype(v_ref.dtype)

        tgq = tgq_ref[r, :] * (ggq_ref[...] * (GQA_HD ** -0.5 * LOG2E))
        gq0 = off(_P_GQ).start
        for hh in range(GQA_HEADS):
            qh = pr_src[:, gq0 + hh * LANES:gq0 + (hh + 1) * LANES]
            q_ref[MLA_HEADS + hh, r, :] = (qh * _rms_scale(qh, LANES) * tgq).astype(q_ref.dtype)
        gk, gk_p = pr_src[:, off(_P_GK)], pr_src[:, off(_P_GK_P)]
        sq = gk * gk
        ss_lo = jnp.sum(jnp.where(low, sq, 0.0), axis=-1, keepdims=True)
        ss_hi = jnp.sum(jnp.where(low, 0.0, sq), axis=-1, keepdims=True)
        rk = jnp.where(low, lax.rsqrt(ss_lo * (1.0 / GQA_HD) + EPS), lax.rsqrt(ss_hi * (1.0 / GQA_HD) + EPS))
        rot = (gk * (ggk_ref[...] * tgc_ref[r, :]) + gk_p * (ggkp_ref[...] * tgs_ref[r, :])) * rk
        rot_x = pltpu.roll(rot, GQA_HD, 1)
        k_ref[MLA_HEADS, r, :] = jnp.where(low, rot, rot_x).astype(k_ref.dtype)
        k_ref[MLA_HEADS + 1, r, :] = jnp.where(low, rot_x, rot).astype(k_ref.dtype)
        v_ref[MLA_PAIRS, r, :] = pr_src[:, off(_P_GV)].astype(v_ref.dtype)

    stage_matmuls(0, ql0, kv0, pr0)

    def trip(it, carry):
        for cur in range(2):
            @pl.when(it % 2 == cur)
            def _():
                stage_matmuls(it + 1, ql[1 - cur], kvs[1 - cur], pr[1 - cur])
                stage_heads(it, ql[cur], kvs[cur], pr[cur])
        return carry

    lax.fori_loop(0, n_items, trip, 0)


def _const_spec(shape):
    return pl.BlockSpec(shape, lambda *_: (0,) * len(shape), pipeline_mode=pl.Buffered(1))


def _projections(x, consts, tables):
    B, S, _ = x.shape
    tb, ts = TB_PROJ, TS_PROJ
    slots = lambda n: pl.BlockSpec((None, n, tb, LANES), lambda b, i: (b, 0, i, 0))
    tab = pl.BlockSpec((tb, LANES), lambda b, i: (i, 0))
    n_slots = (Q_SLOTS, K_SLOTS, V_SLOTS)
    widths = (MLA_HEADS * LANES, MLA_HEADS * (LANES + MLA_V), P_WIDTH - _P_KPE[0])
    return pl.pallas_call(
        _proj_kernel,
        grid=(B, S // tb),
        in_specs=[pl.BlockSpec((None, tb, D_MODEL), lambda b, i: (b, i, 0))]
                 + [_const_spec(c.shape) for c in consts] + [tab] * len(tables),
        out_specs=[slots(n) for n in n_slots],
        out_shape=[jax.ShapeDtypeStruct((B, n, S, LANES), jnp.bfloat16) for n in n_slots],
        scratch_shapes=[pltpu.VMEM((ts, w), jnp.float32) for w in widths for _ in range(2)],
        compiler_params=pltpu.CompilerParams(
            dimension_semantics=("parallel", "parallel"), vmem_limit_bytes=VMEM_LIMIT_BYTES),
        name="projections",
    )(x, *consts, *tables)


def _attn_kernel(q_ref, k_ref, v_ref, o_ref, sc0, sc1, mx0, mx1):
    tq, kv_len = sc0.shape[1:]
    n_items = (q_ref.shape[1] // tq) * N_PAIRS
    sc, mx = (sc0, sc1), (mx0, mx1)
    v_low = lax.broadcasted_iota(jnp.int32, (kv_len, LANES), 1) < V_HALF
    o_low = lax.broadcasted_iota(jnp.int32, (tq, LANES), 1) < V_HALF
    one = jnp.ones((), v_ref.dtype)

    def item(it):
        it = jnp.minimum(it, n_items - 1)
        return pl.multiple_of((it // N_PAIRS) * tq, tq), it % N_PAIRS

    def stage_scores(it, s_dst, m_dst):
        row0, j = item(it)
        mla = j < MLA_PAIRS
        for e in range(2):
            qi = jnp.where(mla, 2 * j + e, MLA_HEADS + j - MLA_PAIRS + e * GQA_GROUP)
            ki = jnp.where(mla, 2 * j + e, MLA_HEADS + e)
            s = lax.dot_general(q_ref[qi, pl.ds(row0, tq), :], k_ref[ki],
                                (((1,), (1,)), ((), ())), preferred_element_type=jnp.float32)
            s_dst[e] = s
            m_dst[e] = jnp.broadcast_to(jnp.max(s, axis=-1, keepdims=True), (tq, LANES))

    def stage_values(it, s_src, m_src):
        row0, j = item(it)
        vi = jnp.minimum(j, MLA_PAIRS)
        v = v_ref[vi]
        accs = []
        for e in range(2):
            m = m_src[e]
            p = jnp.concatenate([jnp.exp2(s_src[e, :, c:c + LANES] - m).astype(v.dtype)
                                 for c in range(0, kv_len, LANES)], axis=-1)
            vh = jnp.where(v_low, v, one) if e == 0 else jnp.where(v_low, one, v)
            accs.append(jnp.dot(p, vh, preferred_element_type=jnp.float32))
        num = jnp.where(o_low, accs[0], accs[1])
        den = jnp.where(o_low, pltpu.roll(accs[0], V_HALF, 1), pltpu.roll(accs[1], V_HALF, 1))
        o_ref[j, pl.ds(row0, tq), :] = (num / den).astype(o_ref.dtype)

    stage_scores(0, sc0, mx0)

    def trip(it, carry):
        for cur in range(2):
            @pl.when(it % 2 == cur)
            def _():
                stage_scores(it + 1, sc[1 - cur], mx[1 - cur])
                stage_values(it, sc[cur], mx[cur])
        return carry

    lax.fori_loop(0, n_items, trip, 0)


def _attention(q, k, v):
    B, _, S, _ = q.shape
    tb, tq = TQ_BLOCK, TQ_ATTN
    pair_buf = lambda w: pltpu.VMEM((2, tq, w), jnp.float32)
    return pl.pallas_call(
        _attn_kernel,
        grid=(B, S // tb),
        in_specs=[pl.BlockSpec((None, Q_SLOTS, tb, LANES), lambda b, i: (b, 0, i, 0)),
                  pl.BlockSpec((None, K_SLOTS, S, LANES), lambda b, i: (b, 0, 0, 0)),
                  pl.BlockSpec((None, V_SLOTS, S, LANES), lambda b, i: (b, 0, 0, 0))],
        out_specs=pl.BlockSpec((None, N_PAIRS, tb, LANES), lambda b, i: (b, 0, i, 0)),
        out_shape=jax.ShapeDtypeStruct((B, N_PAIRS, S, LANES), jnp.bfloat16),
        scratch_shapes=[pair_buf(S), pair_buf(S), pair_buf(LANES), pair_buf(LANES)],
        compiler_params=pltpu.CompilerParams(
            dimension_semantics=("parallel", "parallel"), vmem_limit_bytes=VMEM_LIMIT_BYTES),
        name="attention",
    )(q, k, v)


def _ffn_kernel(x_ref, o_ref, goa_ref, gob_ref, wo_ref, g2_ref, wg_ref, wu_ref, wd_ref, out_ref):
    oa = jnp.concatenate([o_ref[j] for j in range(MLA_PAIRS)], axis=-1).astype(jnp.float32)
    ob = jnp.concatenate([o_ref[j] for j in range(MLA_PAIRS, N_PAIRS)], axis=-1).astype(jnp.float32)
    na = oa * _rms_scale(oa, oa.shape[-1]) * goa_ref[...]
    nb = ob * _rms_scale(ob, ob.shape[-1]) * gob_ref[...]
    mixed = jnp.concatenate([na, nb], axis=-1)
    x1 = x_ref[...] + _bf16_dot(mixed, wo_ref[...])
    h2 = (x1 * _rms_scale(x1, D_MODEL) * g2_ref[...]).astype(jnp.bfloat16)
    g = jnp.dot(h2, wg_ref[...], preferred_element_type=jnp.float32)
    u = jnp.dot(h2, wu_ref[...], preferred_element_type=jnp.float32)
    silu = g / (1.0 + jnp.exp(-g))
    out_ref[...] = x1 + _bf16_dot(silu * u, wd_ref[...])


def _merge_ffn(x, o, consts):
    B, S, _ = x.shape
    tm = TM_FFN
    tok = pl.BlockSpec((None, tm, D_MODEL), lambda b, i: (b, i, 0))
    return pl.pallas_call(
        _ffn_kernel,
        grid=(B, S // tm),
        in_specs=[tok, pl.BlockSpec((None, N_PAIRS, tm, LANES), lambda b, i: (b, 0, i, 0))]
                 + [_const_spec(c.shape) for c in consts],
        out_specs=tok,
        out_shape=jax.ShapeDtypeStruct(x.shape, x.dtype),
        compiler_params=pltpu.CompilerParams(
            dimension_semantics=("parallel", "parallel"), vmem_limit_bytes=VMEM_LIMIT_BYTES),
        name="merge_ffn",
    )(x, o, *consts)


def _rope_tables(seq, d_rot):
    d_ax = d_rot // 2
    inv = ROPE_THETA ** (-(jnp.arange(0, d_ax, 2, dtype=jnp.float32) / d_ax))
    rows = seq // GRID_W
    row = jnp.repeat(jnp.arange(rows, dtype=jnp.float32), GRID_W)
    col = jnp.tile(jnp.arange(GRID_W, dtype=jnp.float32), rows)
    ang_r, ang_c = row[:, None] * inv[None, :], col[:, None] * inv[None, :]
    c = jnp.concatenate([jnp.cos(ang_r)] * 2 + [jnp.cos(ang_c)] * 2, axis=-1)
    s = jnp.concatenate([-jnp.sin(ang_r), jnp.sin(ang_r), -jnp.sin(ang_c), jnp.sin(ang_c)], axis=-1)
    return c, s


def _swap_halves(a):
    d_rot = a.shape[-1]
    a = a.reshape(a.shape[:-1] + (2, 2, d_rot // 4))
    return jnp.flip(a, axis=-2).reshape(a.shape[:-3] + (d_rot,))


def _row(v):
    return v.reshape(1, -1).astype(jnp.float32)


def _pad_lanes(v, front, width):
    return jnp.pad(v, [(0, 0)] * (v.ndim - 1) + [(front, width - front - v.shape[-1])])


def _gqa_slot_order(a, axis):
    shape = a.shape
    a = a.reshape(shape[:axis] + (GQA_KV_HEADS, GQA_GROUP, GQA_HD) + shape[axis + 1:])
    a = jnp.swapaxes(a, axis, axis + 1)
    return a.reshape(shape)


def kernel(x, norm1_g, w_in, q_a_norm_g, w_q_b, kv_a_norm_g, w_kv_b, mla_q_norm_g, mla_k_norm_g,
           gqa_q_norm_g, gqa_k_norm_g, mla_out_norm_g, gqa_out_norm_g, w_o, norm2_g, w_gate, w_up,
           w_down):
    B, S, _ = x.shape
    bf = jnp.bfloat16
    depth = w_in.shape[0]
    assert S % GRID_W == 0 and S % TB_PROJ == 0 and TB_PROJ % TS_PROJ == 0 and S % TM_FFN == 0
    assert S % TQ_BLOCK == 0 and TQ_BLOCK % TQ_ATTN == 0

    cm, sm = _rope_tables(S, MLA_ROPE)
    cg, sg = _rope_tables(S, GQA_HD)
    cat = lambda *parts: jnp.concatenate(parts, axis=-1)
    tables = (cat(jnp.ones((S, MLA_NOPE), jnp.float32), cm, sm),
              cat(cm, sm, cm, sm),
              cat(cg, sg), cat(cg, cg), cat(sg, sg))

    for l in range(depth):
        wi = w_in[l]
        o0 = Q_LORA + KV_LORA
        o1 = o0 + MLA_ROPE
        o2 = o1 + GQA_HEADS * GQA_HD
        o3 = o2 + GQA_KV_HEADS * GQA_HD
        w_kpe, w_gq = wi[:, o0:o1], wi[:, o1:o2].reshape(D_MODEL, GQA_HEADS, GQA_HD)
        w_gk = wi[:, o2:o3].reshape(D_MODEL, GQA_KV_HEADS, GQA_HD)
        win = cat(
            wi[:, :o0],
            w_kpe, _swap_halves(w_kpe), w_kpe, _swap_halves(w_kpe),
            cat(w_gq, _swap_halves(w_gq)).reshape(D_MODEL, GQA_HEADS * LANES),
            w_gk.reshape(D_MODEL, LANES), _swap_halves(w_gk).reshape(D_MODEL, LANES),
            wi[:, o3:]).astype(bf)
        wq = w_q_b[l].reshape(Q_LORA, MLA_HEADS, MLA_QK)
        wqb = cat(wq, _swap_halves(wq[..., MLA_NOPE:])).reshape(Q_LORA, MLA_HEADS * LANES).astype(bf)
        wkv = w_kv_b[l].reshape(KV_LORA, MLA_HEADS, MLA_NOPE + MLA_V)
        wkvb = cat(
            _pad_lanes(wkv[..., :MLA_NOPE], 0, LANES).reshape(KV_LORA, MLA_HEADS * LANES),
            wkv[..., MLA_NOPE:].reshape(KV_LORA, MLA_HEADS * MLA_V)).astype(bf)
        gmq, gmk = mla_q_norm_g[l], mla_k_norm_g[l]
        gmk_pe = gmk[MLA_NOPE:]
        ggq, ggk = gqa_q_norm_g[l], gqa_k_norm_g[l]
        proj_consts = (
            _row(norm1_g[l]), win, _row(q_a_norm_g[l]), wqb, _row(kv_a_norm_g[l]), wkvb,
            _row(cat(gmq, _swap_halves(gmq[MLA_NOPE:]))),
            _row(_pad_lanes(gmk[:MLA_NOPE], 0, LANES)),
            _row(cat(gmk_pe, _swap_halves(gmk_pe), gmk_pe, _swap_halves(gmk_pe))),
            _row(cat(ggq, _swap_halves(ggq))),
            _row(cat(ggk, ggk)), _row(cat(_swap_halves(ggk), _swap_halves(ggk))))
        q, k, v = _projections(x, proj_consts, tables)

        o = _attention(q, k, v)

        n_a = MLA_HEADS * MLA_V
        wo = jnp.concatenate([w_o[l][:n_a], _gqa_slot_order(w_o[l][n_a:], 0)], axis=0).astype(bf)
        ffn_consts = (
            _row(mla_out_norm_g[l]), _row(_gqa_slot_order(gqa_out_norm_g[l], 0)), wo,
            _row(norm2_g[l]), w_gate[l].astype(bf), w_up[l].astype(bf), w_down[l].astype(bf))
        x = _merge_ffn(x, o, ffn_consts)
    return x
```

```python
import math

import jax
import jax.numpy as jnp
import numpy as np
from jax import lax
from jax.experimental import pallas as pl
from jax.experimental.pallas import tpu as pltpu

D_MODEL = 1024
GRID_W = 64
ROPE_THETA = 10000.0
EPS = 1e-6

MLA_HEADS = 8
Q_LORA = 384
KV_LORA = 256
MLA_NOPE = 64
MLA_ROPE = 32
MLA_V = 64
MLA_QK = MLA_NOPE + MLA_ROPE

GQA_HEADS = 8
GQA_KV_HEADS = 2
GQA_HD = 64
GQA_GROUP = GQA_HEADS // GQA_KV_HEADS

D_FF = 2816

LANES = 128
V_HALF = 64
LOG2E = math.log2(math.e)
VMEM_LIMIT_BYTES = 56 * 1024 * 1024

TB_PROJ = 1024
TS_PROJ = 256
TQ_BLOCK = 1024
TQ_ATTN = 512
TM_FFN = 512

MLA_PAIRS = MLA_HEADS // 2
N_PAIRS = MLA_PAIRS + GQA_GROUP
Q_SLOTS = MLA_HEADS + GQA_HEADS
K_SLOTS = MLA_HEADS + GQA_KV_HEADS
V_SLOTS = MLA_PAIRS + 1

_P_CQ = (0, Q_LORA)
_P_CKV = (_P_CQ[1], _P_CQ[1] + KV_LORA)
_P_KPE = (_P_CKV[1], _P_CKV[1] + LANES)
_P_GQ = (_P_KPE[1], _P_KPE[1] + GQA_HEADS * LANES)
_P_GK = (_P_GQ[1], _P_GQ[1] + LANES)
_P_GK_P = (_P_GK[1], _P_GK[1] + LANES)
_P_GV = (_P_GK_P[1], _P_GK_P[1] + LANES)
P_WIDTH = _P_GV[1]


def _bf16_dot(a, b):
    return jnp.dot(a.astype(jnp.bfloat16), b, preferred_element_type=jnp.float32)


def _rms_scale(x, width):
    return lax.rsqrt(jnp.sum(x * x, axis=-1, keepdims=True) * (1.0 / width) + EPS)


def _proj_kernel(x_ref, g1_ref, win_ref, gcq_ref, wqb_ref, gckv_ref, wkvb_ref,
                 gmq_ref, gmkn_ref, gmkp_ref, ggq_ref, ggk_ref, ggkp_ref,
                 tmq_ref, tmk_ref, tgq_ref, tgc_ref, tgs_ref,
                 q_ref, k_ref, v_ref, ql0, ql1, kv0, kv1, pr0, pr1):
    ts = ql0.shape[0]
    n_items = x_ref.shape[0] // ts
    ql, kvs, pr = (ql0, ql1), (kv0, kv1), (pr0, pr1)

    def rows(it):
        return pl.ds(pl.multiple_of(jnp.minimum(it, n_items - 1) * ts, ts), ts)

    def stage_matmuls(it, ql_dst, kv_dst, pr_dst):
        x = x_ref[rows(it), :]
        h = x * _rms_scale(x, D_MODEL) * g1_ref[...]
        p = _bf16_dot(h, win_ref[...])
        cq = p[:, _P_CQ[0]:_P_CQ[1]]
        ckv = p[:, _P_CKV[0]:_P_CKV[1]]
        ql_dst[...] = _bf16_dot(cq * _rms_scale(cq, Q_LORA) * gcq_ref[...], wqb_ref[...])
        kv_dst[...] = _bf16_dot(ckv * _rms_scale(ckv, KV_LORA) * gckv_ref[...], wkvb_ref[...])
        pr_dst[...] = p[:, _P_KPE[0]:]

    def stage_heads(it, ql_src, kv_src, pr_src):
        r = rows(it)
        off = lambda span: slice(span[0] - _P_KPE[0], span[1] - _P_KPE[0])
        lane = lax.broadcasted_iota(jnp.int32, (ts, LANES), 1)
        low = lane < LANES // 2

        y = pr_src[:, off(_P_KPE)] * gmkp_ref[...] * tmk_ref[r, :]
        kpe_rot = y + pltpu.roll(y, MLA_ROPE, 1)
        kpe = jnp.where(lane < MLA_ROPE, pr_src[:, off(_P_KPE)], 0.0)
        ss_pe = jnp.sum(kpe * kpe, axis=-1, keepdims=True)
        tmq = tmq_ref[r, :] * (gmq_ref[...] * (MLA_QK ** -0.5 * LOG2E))
        gmkn = gmkn_ref[...]
        for hh in range(MLA_HEADS):
            sl = slice(hh * LANES, (hh + 1) * LANES)
            qh = ql_src[:, sl]
            qsq = jnp.where(lane < MLA_QK, qh * qh, 0.0)
            rq = lax.rsqrt(jnp.sum(qsq, axis=-1, keepdims=True) * (1.0 / MLA_QK) + EPS)
            q_ref[hh, r, :] = (qh * rq * tmq).astype(q_ref.dtype)
            kn = kv_src[:, sl]
            rk = lax.rsqrt((jnp.sum(kn * kn, axis=-1, keepdims=True) + ss_pe) * (1.0 / MLA_QK) + EPS)
            k_ref[hh, r, :] = (jnp.where(low, kn * gmkn, kpe_rot) * rk).astype(k_ref.dtype)
        for j in range(MLA_PAIRS):
            lo = MLA_HEADS * LANES + j * LANES
            v_ref[j, r, :] = kv_src[:, lo:lo + LANES].astype(v_ref.dtype)

        tgq = tgq_ref[r, :] * (ggq_ref[...] * (GQA_HD ** -0.5 * LOG2E))
        gq0 = off(_P_GQ).start
        for hh in range(GQA_HEADS):
            qh = pr_src[:, gq0 + hh * LANES:gq0 + (hh + 1) * LANES]
            q_ref[MLA_HEADS + hh, r, :] = (qh * _rms_scale(qh, LANES) * tgq).astype(q_ref.dtype)
        gk, gk_p = pr_src[:, off(_P_GK)], pr_src[:, off(_P_GK_P)]
        sq = gk * gk
        ss_lo = jnp.sum(jnp.where(low, sq, 0.0), axis=-1, keepdims=True)
        ss_hi = jnp.sum(jnp.where(low, 0.0, sq), axis=-1, keepdims=True)
        rk = jnp.where(low, lax.rsqrt(ss_lo * (1.0 / GQA_HD) + EPS), lax.rsqrt(ss_hi * (1.0 / GQA_HD) + EPS))
        rot = (gk * (ggk_ref[...] * tgc_ref[r, :]) + gk_p * (ggkp_ref[...] * tgs_ref[r, :])) * rk
        rot_x = pltpu.roll(rot, GQA_HD, 1)
        k_ref[MLA_HEADS, r, :] = jnp.where(low, rot, rot_x).astype(k_ref.dtype)
        k_ref[MLA_HEADS + 1, r, :] = jnp.where(low, rot_x, rot).astype(k_ref.dtype)
        v_ref[MLA_PAIRS, r, :] = pr_src[:, off(_P_GV)].astype(v_ref.dtype)

    stage_matmuls(0, ql0, kv0, pr0)

    def trip(it, carry):
        for cur in range(2):
            @pl.when(it % 2 == cur)
            def _():
                stage_matmuls(it + 1, ql[1 - cur], kvs[1 - cur], pr[1 - cur])
                stage_heads(it, ql[cur], kvs[cur], pr[cur])
        return carry

    lax.fori_loop(0, n_items, trip, 0)


def _const_spec(shape):
    return pl.BlockSpec(shape, lambda *_: (0,) * len(shape), pipeline_mode=pl.Buffered(1))


def _projections(x, consts, tables):
    B, S, _ = x.shape
    tb, ts = TB_PROJ, TS_PROJ
    slots = lambda n: pl.BlockSpec((None, n, tb, LANES), lambda b, i: (b, 0, i, 0))
    tab = pl.BlockSpec((tb, LANES), lambda b, i: (i, 0))
    n_slots = (Q_SLOTS, K_SLOTS, V_SLOTS)
    widths = (MLA_HEADS * LANES, MLA_HEADS * (LANES + MLA_V), P_WIDTH - _P_KPE[0])
    return pl.pallas_call(
        _proj_kernel,
        grid=(B, S // tb),
        in_specs=[pl.BlockSpec((None, tb, D_MODEL), lambda b, i: (b, i, 0))]
                 + [_const_spec(c.shape) for c in consts] + [tab] * len(tables),
        out_specs=[slots(n) for n in n_slots],
        out_shape=[jax.ShapeDtypeStruct((B, n, S, LANES), jnp.bfloat16) for n in n_slots],
        scratch_shapes=[pltpu.VMEM((ts, w), jnp.float32) for w in widths for _ in range(2)],
        compiler_params=pltpu.CompilerParams(
            dimension_semantics=("parallel", "parallel"), vmem_limit_bytes=VMEM_LIMIT_BYTES),
        name="projections",
    )(x, *consts, *tables)


def _attn_kernel(q_ref, k_ref, v_ref, o_ref, sc0, sc1, mx0, mx1):
    tq, kv_len = sc0.shape[1:]
    n_items = (q_ref.shape[1] // tq) * N_PAIRS
    sc, mx = (sc0, sc1), (mx0, mx1)
    v_low = lax.broadcasted_iota(jnp.int32, (kv_len, LANES), 1) < V_HALF
    o_low = lax.broadcasted_iota(jnp.int32, (tq, LANES), 1) < V_HALF
    one = jnp.ones((), v_ref.dtype)

    def item(it):
        it = jnp.minimum(it, n_items - 1)
        return pl.multiple_of((it // N_PAIRS) * tq, tq), it % N_PAIRS

    def stage_scores(it, s_dst, m_dst):
        row0, j = item(it)
        mla = j < MLA_PAIRS
        for e in range(2):
            qi = jnp.where(mla, 2 * j + e, MLA_HEADS + j - MLA_PAIRS + e * GQA_GROUP)
            ki = jnp.where(mla, 2 * j + e, MLA_HEADS + e)
            s = lax.dot_general(q_ref[qi, pl.ds(row0, tq), :], k_ref[ki],
                                (((1,), (1,)), ((), ())), preferred_element_type=jnp.float32)
            s_dst[e] = s
            m_dst[e] = jnp.broadcast_to(jnp.max(s, axis=-1, keepdims=True), (tq, LANES))

    def stage_values(it, s_src, m_src):
        row0, j = item(it)
        vi = jnp.minimum(j, MLA_PAIRS)
        v = v_ref[vi]
        accs = []
        for e in range(2):
            m = m_src[e]
            p = jnp.concatenate([jnp.exp2(s_src[e, :, c:c + LANES] - m).astype(v.dtype)
                                 for c in range(0, kv_len, LANES)], axis=-1)
            vh = jnp.where(v_low, v, one) if e == 0 else jnp.where(v_low, one, v)
            accs.append(jnp.dot(p, vh, preferred_element_type=jnp.float32))
        num = jnp.where(o_low, accs[0], accs[1])
        den = jnp.where(o_low, pltpu.roll(accs[0], V_HALF, 1), pltpu.roll(accs[1], V_HALF, 1))
        o_ref[j, pl.ds(row0, tq), :] = (num / den).astype(o_ref.dtype)

    stage_scores(0, sc0, mx0)
    stage_scores(1, sc1, mx1)

    def trip(t, carry):
        stage_values(2 * t, sc0, mx0)
        stage_values(2 * t + 1, sc1, mx1)
        stage_scores(2 * t + 2, sc0, mx0)
        stage_scores(2 * t + 3, sc1, mx1)
        return carry

    lax.fori_loop(0, n_items // 2 - 1, trip, 0)
    stage_values(n_items - 2, sc0, mx0)
    stage_values(n_items - 1, sc1, mx1)


def _attention(q, k, v):
    B, _, S, _ = q.shape
    tb, tq = TQ_BLOCK, TQ_ATTN
    pair_buf = lambda w: pltpu.VMEM((2, tq, w), jnp.float32)
    return pl.pallas_call(
        _attn_kernel,
        grid=(B, S // tb),
        in_specs=[pl.BlockSpec((None, Q_SLOTS, tb, LANES), lambda b, i: (b, 0, i, 0)),
                  pl.BlockSpec((None, K_SLOTS, S, LANES), lambda b, i: (b, 0, 0, 0)),
                  pl.BlockSpec((None, V_SLOTS, S, LANES), lambda b, i: (b, 0, 0, 0))],
        out_specs=pl.BlockSpec((None, N_PAIRS, tb, LANES), lambda b, i: (b, 0, i, 0)),
        out_shape=jax.ShapeDtypeStruct((B, N_PAIRS, S, LANES), jnp.bfloat16),
        scratch_shapes=[pair_buf(S), pair_buf(S), pair_buf(LANES), pair_buf(LANES)],
        compiler_params=pltpu.CompilerParams(
            dimension_semantics=("parallel", "parallel"), vmem_limit_bytes=VMEM_LIMIT_BYTES),
        name="attention",
    )(q, k, v)


def _ffn_kernel(x_ref, o_ref, goa_ref, gob_ref, wo_ref, g2_ref, wg_ref, wu_ref, wd_ref, out_ref):
    oa = jnp.concatenate([o_ref[j] for j in range(MLA_PAIRS)], axis=-1).astype(jnp.float32)
    ob = jnp.concatenate([o_ref[j] for j in range(MLA_PAIRS, N_PAIRS)], axis=-1).astype(jnp.float32)
    na = oa * _rms_scale(oa, oa.shape[-1]) * goa_ref[...]
    nb = ob * _rms_scale(ob, ob.shape[-1]) * gob_ref[...]
    mixed = jnp.concatenate([na, nb], axis=-1)
    x1 = x_ref[...] + _bf16_dot(mixed, wo_ref[...])
    h2 = (x1 * _rms_scale(x1, D_MODEL) * g2_ref[...]).astype(jnp.bfloat16)
    g = jnp.dot(h2, wg_ref[...], preferred_element_type=jnp.float32)
    u = jnp.dot(h2, wu_ref[...], preferred_element_type=jnp.float32)
    silu = g / (1.0 + jnp.exp(-g))
    out_ref[...] = x1 + _bf16_dot(silu * u, wd_ref[...])


def _merge_ffn(x, o, consts):
    B, S, _ = x.shape
    tm = TM_FFN
    tok = pl.BlockSpec((None, tm, D_MODEL), lambda b, i: (b, i, 0))
    return pl.pallas_call(
        _ffn_kernel,
        grid=(B, S // tm),
        in_specs=[tok, pl.BlockSpec((None, N_PAIRS, tm, LANES), lambda b, i: (b, 0, i, 0))]
                 + [_const_spec(c.shape) for c in consts],
        out_specs=tok,
        out_shape=jax.ShapeDtypeStruct(x.shape, x.dtype),
        compiler_params=pltpu.CompilerParams(
            dimension_semantics=("parallel", "parallel"), vmem_limit_bytes=VMEM_LIMIT_BYTES),
        name="merge_ffn",
    )(x, o, *consts)


def _rope_tables(seq, d_rot):
    d_ax = d_rot // 2
    inv = np.float32(ROPE_THETA) ** (-(np.arange(0, d_ax, 2, dtype=np.float32) / np.float32(d_ax)))
    rows = seq // GRID_W
    row = np.repeat(np.arange(rows, dtype=np.float32), GRID_W)
    col = np.tile(np.arange(GRID_W, dtype=np.float32), rows)
    ang_r, ang_c = row[:, None] * inv[None, :], col[:, None] * inv[None, :]
    c = np.concatenate([np.cos(ang_r)] * 2 + [np.cos(ang_c)] * 2, axis=-1)
    s = np.concatenate([-np.sin(ang_r), np.sin(ang_r), -np.sin(ang_c), np.sin(ang_c)], axis=-1)
    return c.astype(np.float32), s.astype(np.float32)


def _swap_halves(a):
    d_rot = a.shape[-1]
    a = a.reshape(a.shape[:-1] + (2, 2, d_rot // 4))
    return jnp.flip(a, axis=-2).reshape(a.shape[:-3] + (d_rot,))


def _row(v):
    return v.reshape(1, -1).astype(jnp.float32)


def _pad_lanes(v, front, width):
    return jnp.pad(v, [(0, 0)] * (v.ndim - 1) + [(front, width - front - v.shape[-1])])


def _gqa_slot_order(a, axis):
    shape = a.shape
    a = a.reshape(shape[:axis] + (GQA_KV_HEADS, GQA_GROUP, GQA_HD) + shape[axis + 1:])
    a = jnp.swapaxes(a, axis, axis + 1)
    return a.reshape(shape)


def kernel(x, norm1_g, w_in, q_a_norm_g, w_q_b, kv_a_norm_g, w_kv_b, mla_q_norm_g, mla_k_norm_g,
           gqa_q_norm_g, gqa_k_norm_g, mla_out_norm_g, gqa_out_norm_g, w_o, norm2_g, w_gate, w_up,
           w_down):
    B, S, _ = x.shape
    bf = jnp.bfloat16
    depth = w_in.shape[0]
    assert S % GRID_W == 0 and S % TB_PROJ == 0 and TB_PROJ % TS_PROJ == 0 and S % TM_FFN == 0
    assert S % TQ_BLOCK == 0 and TQ_BLOCK % TQ_ATTN == 0

    cm, sm = _rope_tables(S, MLA_ROPE)
    cg, sg = _rope_tables(S, GQA_HD)
    ncat = lambda *parts: jnp.asarray(np.concatenate(parts, axis=-1))
    tables = (ncat(np.ones((S, MLA_NOPE), np.float32), cm, sm),
              ncat(cm, sm, cm, sm),
              ncat(cg, sg), ncat(cg, cg), ncat(sg, sg))
    cat = lambda *parts: jnp.concatenate(parts, axis=-1)

    for l in range(depth):
        wi = w_in[l]
        o0 = Q_LORA + KV_LORA
        o1 = o0 + MLA_ROPE
        o2 = o1 + GQA_HEADS * GQA_HD
        o3 = o2 + GQA_KV_HEADS * GQA_HD
        w_kpe, w_gq = wi[:, o0:o1], wi[:, o1:o2].reshape(D_MODEL, GQA_HEADS, GQA_HD)
        w_gk = wi[:, o2:o3].reshape(D_MODEL, GQA_KV_HEADS, GQA_HD)
        win = cat(
            wi[:, :o0],
            w_kpe, _swap_halves(w_kpe), w_kpe, _swap_halves(w_kpe),
            cat(w_gq, _swap_halves(w_gq)).reshape(D_MODEL, GQA_HEADS * LANES),
            w_gk.reshape(D_MODEL, LANES), _swap_halves(w_gk).reshape(D_MODEL, LANES),
            wi[:, o3:]).astype(bf)
        wq = w_q_b[l].reshape(Q_LORA, MLA_HEADS, MLA_QK)
        wqb = cat(wq, _swap_halves(wq[..., MLA_NOPE:])).reshape(Q_LORA, MLA_HEADS * LANES).astype(bf)
        wkv = w_kv_b[l].reshape(KV_LORA, MLA_HEADS, MLA_NOPE + MLA_V)
        wkvb = cat(
            _pad_lanes(wkv[..., :MLA_NOPE], 0, LANES).reshape(KV_LORA, MLA_HEADS * LANES),
            wkv[..., MLA_NOPE:].reshape(KV_LORA, MLA_HEADS * MLA_V)).astype(bf)
        gmq, gmk = mla_q_norm_g[l], mla_k_norm_g[l]
        gmk_pe = gmk[MLA_NOPE:]
        ggq, ggk = gqa_q_norm_g[l], gqa_k_norm_g[l]
        proj_consts = (
            _row(norm1_g[l]), win, _row(q_a_norm_g[l]), wqb, _row(kv_a_norm_g[l]), wkvb,
            _row(cat(gmq, _swap_halves(gmq[MLA_NOPE:]))),
            _row(_pad_lanes(gmk[:MLA_NOPE], 0, LANES)),
            _row(cat(gmk_pe, _swap_halves(gmk_pe), gmk_pe, _swap_halves(gmk_pe))),
            _row(cat(ggq, _swap_halves(ggq))),
            _row(cat(ggk, ggk)), _row(cat(_swap_halves(ggk), _swap_halves(ggk))))
        q, k, v = _projections(x, proj_consts, tables)

        o = _attention(q, k, v)

        n_a = MLA_HEADS * MLA_V
        wo = jnp.concatenate([w_o[l][:n_a], _gqa_slot_order(w_o[l][n_a:], 0)], axis=0).astype(bf)
        ffn_consts = (
            _row(mla_out_norm_g[l]), _row(_gqa_slot_order(gqa_out_norm_g[l], 0)), wo,
            _row(norm2_g[l]), w_gate[l].astype(bf), w_up[l].astype(bf), w_down[l].astype(bf))
        x = _merge_ffn(x, o, ffn_consts)
    return x
```

```python
import math

import jax
import jax.numpy as jnp
import numpy as np
from jax import lax
from jax.experimental import pallas as pl
from jax.experimental.pallas import tpu as pltpu

D_MODEL = 1024
GRID_W = 64
ROPE_THETA = 10000.0
EPS = 1e-6

MLA_HEADS = 8
Q_LORA = 384
KV_LORA = 256
MLA_NOPE = 64
MLA_ROPE = 32
MLA_V = 64
MLA_QK = MLA_NOPE + MLA_ROPE

GQA_HEADS = 8
GQA_KV_HEADS = 2
GQA_HD = 64
GQA_GROUP = GQA_HEADS // GQA_KV_HEADS

D_FF = 2816

LANES = 128
V_HALF = 64
LOG2E = math.log2(math.e)
VMEM_LIMIT_BYTES = 56 * 1024 * 1024

TB_PROJ = 1024
TS_PROJ = 256
TQ_BLOCK = 1024
TQ_ATTN = 512
TM_FFN = 512

MLA_PAIRS = MLA_HEADS // 2
N_PAIRS = MLA_PAIRS + GQA_GROUP
Q_SLOTS = MLA_HEADS + GQA_HEADS
K_SLOTS = MLA_HEADS + GQA_KV_HEADS
V_SLOTS = MLA_PAIRS + 1

_P_CQ = (0, Q_LORA)
_P_CKV = (_P_CQ[1], _P_CQ[1] + KV_LORA)
_P_KPE = (_P_CKV[1], _P_CKV[1] + LANES)
_P_GQ = (_P_KPE[1], _P_KPE[1] + GQA_HEADS * LANES)
_P_GK = (_P_GQ[1], _P_GQ[1] + LANES)
_P_GK_P = (_P_GK[1], _P_GK[1] + LANES)
_P_GV = (_P_GK_P[1], _P_GK_P[1] + LANES)
P_WIDTH = _P_GV[1]


def _bf16_dot(a, b):
    return jnp.dot(a.astype(jnp.bfloat16), b, preferred_element_type=jnp.float32)


def _rms_scale(x, width):
    return lax.rsqrt(jnp.sum(x * x, axis=-1, keepdims=True) * (1.0 / width) + EPS)


def _proj_kernel(x_ref, xn_ref, g1_ref, win_ref, gcq_ref, wqb_ref, gckv_ref, wkvb_ref,
                 gmq_ref, gmkn_ref, gmkp_ref, ggq_ref, ggk_ref, ggkp_ref,
                 tmq_ref, tmk_ref, tgq_ref, tgc_ref, tgs_ref,
                 q_ref, k_ref, v_ref, ql0, ql1, kv0, kv1, pr0, pr1):
    ts = ql0.shape[0]
    n_items = x_ref.shape[0] // ts
    assert n_items % 2 == 0
    ql, kvs, pr = (ql0, ql1), (kv0, kv1), (pr0, pr1)

    def rows(it):
        return pl.ds(pl.multiple_of(it * ts, ts), ts)

    def stage_matmuls(src_ref, it, ql_dst, kv_dst, pr_dst):
        x = src_ref[rows(it), :]
        h = x * _rms_scale(x, D_MODEL) * g1_ref[...]
        p = _bf16_dot(h, win_ref[...])
        cq = p[:, _P_CQ[0]:_P_CQ[1]]
        ckv = p[:, _P_CKV[0]:_P_CKV[1]]
        ql_dst[...] = _bf16_dot(cq * _rms_scale(cq, Q_LORA) * gcq_ref[...], wqb_ref[...])
        kv_dst[...] = _bf16_dot(ckv * _rms_scale(ckv, KV_LORA) * gckv_ref[...], wkvb_ref[...])
        pr_dst[...] = p[:, _P_KPE[0]:]

    def stage_heads(it, ql_src, kv_src, pr_src):
        r = rows(it)
        off = lambda span: slice(span[0] - _P_KPE[0], span[1] - _P_KPE[0])
        lane = lax.broadcasted_iota(jnp.int32, (ts, LANES), 1)
        low = lane < LANES // 2

        y = pr_src[:, off(_P_KPE)] * gmkp_ref[...] * tmk_ref[r, :]
        kpe_rot = y + pltpu.roll(y, MLA_ROPE, 1)
        kpe = jnp.where(lane < MLA_ROPE, pr_src[:, off(_P_KPE)], 0.0)
        ss_pe = jnp.sum(kpe * kpe, axis=-1, keepdims=True)
        tmq = tmq_ref[r, :] * (gmq_ref[...] * (MLA_QK ** -0.5 * LOG2E))
        gmkn = gmkn_ref[...]
        for hh in range(MLA_HEADS):
            sl = slice(hh * LANES, (hh + 1) * LANES)
            qh = ql_src[:, sl]
            qsq = jnp.where(lane < MLA_QK, qh * qh, 0.0)
            rq = lax.rsqrt(jnp.sum(qsq, axis=-1, keepdims=True) * (1.0 / MLA_QK) + EPS)
            q_ref[hh, r, :] = (qh * rq * tmq).astype(q_ref.dtype)
            kn = kv_src[:, sl]
            rk = lax.rsqrt((jnp.sum(kn * kn, axis=-1, keepdims=True) + ss_pe) * (1.0 / MLA_QK) + EPS)
            k_ref[hh, r, :] = (jnp.where(low, kn * gmkn, kpe_rot) * rk).astype(k_ref.dtype)
        for j in range(MLA_PAIRS):
            lo = MLA_HEADS * LANES + j * LANES
            v_ref[j, r, :] = kv_src[:, lo:lo + LANES].astype(v_ref.dtype)

        tgq = tgq_ref[r, :] * (ggq_ref[...] * (GQA_HD ** -0.5 * LOG2E))
        gq0 = off(_P_GQ).start
        for hh in range(GQA_HEADS):
            qh = pr_src[:, gq0 + hh * LANES:gq0 + (hh + 1) * LANES]
            q_ref[MLA_HEADS + hh, r, :] = (qh * _rms_scale(qh, LANES) * tgq).astype(q_ref.dtype)
        gk, gk_p = pr_src[:, off(_P_GK)], pr_src[:, off(_P_GK_P)]
        sq = gk * gk
        ss_lo = jnp.sum(jnp.where(low, sq, 0.0), axis=-1, keepdims=True)
        ss_hi = jnp.sum(jnp.where(low, 0.0, sq), axis=-1, keepdims=True)
        rk = jnp.where(low, lax.rsqrt(ss_lo * (1.0 / GQA_HD) + EPS), lax.rsqrt(ss_hi * (1.0 / GQA_HD) + EPS))
        rot = (gk * (ggk_ref[...] * tgc_ref[r, :]) + gk_p * (ggkp_ref[...] * tgs_ref[r, :])) * rk
        rot_x = pltpu.roll(rot, GQA_HD, 1)
        k_ref[MLA_HEADS, r, :] = jnp.where(low, rot, rot_x).astype(k_ref.dtype)
        k_ref[MLA_HEADS + 1, r, :] = jnp.where(low, rot_x, rot).astype(k_ref.dtype)
        v_ref[MLA_PAIRS, r, :] = pr_src[:, off(_P_GV)].astype(v_ref.dtype)

    @pl.when((pl.program_id(0) == 0) & (pl.program_id(1) == 0))
    def _():
        stage_matmuls(x_ref, 0, ql0, kv0, pr0)

    def trip(it, carry):
        for cur in range(2):
            @pl.when(it % 2 == cur)
            def _():
                stage_matmuls(x_ref, it + 1, ql[1 - cur], kvs[1 - cur], pr[1 - cur])
                stage_heads(it, ql[cur], kvs[cur], pr[cur])
        return carry

    lax.fori_loop(0, n_items - 1, trip, 0)
    stage_matmuls(xn_ref, 0, ql0, kv0, pr0)
    stage_heads(n_items - 1, ql1, kv1, pr1)


def _const_spec(shape):
    return pl.BlockSpec(shape, lambda *_: (0,) * len(shape), pipeline_mode=pl.Buffered(1))


def _projections(x, consts, tables):
    B, S, _ = x.shape
    tb, ts = TB_PROJ, TS_PROJ
    slots = lambda n: pl.BlockSpec((None, n, tb, LANES), lambda b, i: (b, 0, i, 0))
    tab = pl.BlockSpec((tb, LANES), lambda b, i: (i, 0))
    n_slots = (Q_SLOTS, K_SLOTS, V_SLOTS)
    widths = (MLA_HEADS * LANES, MLA_HEADS * (LANES + MLA_V), P_WIDTH - _P_KPE[0])
    n_i = S // tb

    def next_block(b, i):
        t = jnp.minimum(b * n_i + i + 1, B * n_i - 1)
        return t // n_i, t % n_i, 0

    return pl.pallas_call(
        _proj_kernel,
        grid=(B, n_i),
        in_specs=[pl.BlockSpec((None, tb, D_MODEL), lambda b, i: (b, i, 0)),
                  pl.BlockSpec((None, tb, D_MODEL), next_block)]
                 + [_const_spec(c.shape) for c in consts] + [tab] * len(tables),
        out_specs=[slots(n) for n in n_slots],
        out_shape=[jax.ShapeDtypeStruct((B, n, S, LANES), jnp.bfloat16) for n in n_slots],
        scratch_shapes=[pltpu.VMEM((ts, w), jnp.float32) for w in widths for _ in range(2)],
        compiler_params=pltpu.CompilerParams(
            dimension_semantics=("arbitrary", "arbitrary"), vmem_limit_bytes=VMEM_LIMIT_BYTES),
        name="projections",
    )(x, x, *consts, *tables)


def _attn_kernel(q_ref, k_ref, v_ref, o_ref, sc0, sc1, mx0, mx1):
    tq, kv_len = sc0.shape[1:]
    n_items = (q_ref.shape[1] // tq) * N_PAIRS
    sc, mx = (sc0, sc1), (mx0, mx1)
    v_low = lax.broadcasted_iota(jnp.int32, (kv_len, LANES), 1) < V_HALF
    o_low = lax.broadcasted_iota(jnp.int32, (tq, LANES), 1) < V_HALF
    one = jnp.ones((), v_ref.dtype)

    def item(it):
        it = jnp.minimum(it, n_items - 1)
        return pl.multiple_of((it // N_PAIRS) * tq, tq), it % N_PAIRS

    def stage_scores(it, s_dst, m_dst):
        row0, j = item(it)
        mla = j < MLA_PAIRS
        for e in range(2):
            qi = jnp.where(mla, 2 * j + e, MLA_HEADS + j - MLA_PAIRS + e * GQA_GROUP)
            ki = jnp.where(mla, 2 * j + e, MLA_HEADS + e)
            s = lax.dot_general(q_ref[qi, pl.ds(row0, tq), :], k_ref[ki],
                                (((1,), (1,)), ((), ())), preferred_element_type=jnp.float32)
            s_dst[e] = s
            m_dst[e] = jnp.broadcast_to(jnp.max(s, axis=-1, keepdims=True), (tq, LANES))

    def stage_values(it, s_src, m_src):
        row0, j = item(it)
        vi = jnp.minimum(j, MLA_PAIRS)
        v = v_ref[vi]
        accs = []
        for e in range(2):
            m = m_src[e]
            p = jnp.concatenate([jnp.exp2(s_src[e, :, c:c + LANES] - m).astype(v.dtype)
                                 for c in range(0, kv_len, LANES)], axis=-1)
            vh = jnp.where(v_low, v, one) if e == 0 else jnp.where(v_low, one, v)
            accs.append(jnp.dot(p, vh, preferred_element_type=jnp.float32))
        num = jnp.where(o_low, accs[0], accs[1])
        den = jnp.where(o_low, pltpu.roll(accs[0], V_HALF, 1), pltpu.roll(accs[1], V_HALF, 1))
        o_ref[j, pl.ds(row0, tq), :] = (num / den).astype(o_ref.dtype)

    stage_scores(0, sc0, mx0)
    stage_scores(1, sc1, mx1)

    def trip(t, carry):
        stage_values(2 * t, sc0, mx0)
        stage_values(2 * t + 1, sc1, mx1)
        stage_scores(2 * t + 2, sc0, mx0)
        stage_scores(2 * t + 3, sc1, mx1)
        return carry

    lax.fori_loop(0, n_items // 2 - 1, trip, 0)
    stage_values(n_items - 2, sc0, mx0)
    stage_values(n_items - 1, sc1, mx1)


def _attention(q, k, v):
    B, _, S, _ = q.shape
    tb, tq = TQ_BLOCK, TQ_ATTN
    pair_buf = lambda w: pltpu.VMEM((2, tq, w), jnp.float32)
    return pl.pallas_call(
        _attn_kernel,
        grid=(B, S // tb),
        in_specs=[pl.BlockSpec((None, Q_SLOTS, tb, LANES), lambda b, i: (b, 0, i, 0)),
                  pl.BlockSpec((None, K_SLOTS, S, LANES), lambda b, i: (b, 0, 0, 0)),
                  pl.BlockSpec((None, V_SLOTS, S, LANES), lambda b, i: (b, 0, 0, 0))],
        out_specs=pl.BlockSpec((None, N_PAIRS, tb, LANES), lambda b, i: (b, 0, i, 0)),
        out_shape=jax.ShapeDtypeStruct((B, N_PAIRS, S, LANES), jnp.bfloat16),
        scratch_shapes=[pair_buf(S), pair_buf(S), pair_buf(LANES), pair_buf(LANES)],
        compiler_params=pltpu.CompilerParams(
            dimension_semantics=("parallel", "parallel"), vmem_limit_bytes=VMEM_LIMIT_BYTES),
        name="attention",
    )(q, k, v)


def _ffn_kernel(x_ref, o_ref, goa_ref, gob_ref, wo_ref, g2_ref, wg_ref, wu_ref, wd_ref, out_ref):
    oa = jnp.concatenate([o_ref[j] for j in range(MLA_PAIRS)], axis=-1).astype(jnp.float32)
    ob = jnp.concatenate([o_ref[j] for j in range(MLA_PAIRS, N_PAIRS)], axis=-1).astype(jnp.float32)
    na = oa * _rms_scale(oa, oa.shape[-1]) * goa_ref[...]
    nb = ob * _rms_scale(ob, ob.shape[-1]) * gob_ref[...]
    mixed = jnp.concatenate([na, nb], axis=-1)
    x1 = x_ref[...] + _bf16_dot(mixed, wo_ref[...])
    h2 = (x1 * _rms_scale(x1, D_MODEL) * g2_ref[...]).astype(jnp.bfloat16)
    g = jnp.dot(h2, wg_ref[...], preferred_element_type=jnp.float32)
    u = jnp.dot(h2, wu_ref[...], preferred_element_type=jnp.float32)
    silu = g / (1.0 + jnp.exp(-g))
    out_ref[...] = x1 + _bf16_dot(silu * u, wd_ref[...])


def _merge_ffn(x, o, consts):
    B, S, _ = x.shape
    tm = TM_FFN
    tok = pl.BlockSpec((None, tm, D_MODEL), lambda b, i: (b, i, 0))
    return pl.pallas_call(
        _ffn_kernel,
        grid=(B, S // tm),
        in_specs=[tok, pl.BlockSpec((None, N_PAIRS, tm, LANES), lambda b, i: (b, 0, i, 0))]
                 + [_const_spec(c.shape) for c in consts],
        out_specs=tok,
        out_shape=jax.ShapeDtypeStruct(x.shape, x.dtype),
        compiler_params=pltpu.CompilerParams(
            dimension_semantics=("parallel", "parallel"), vmem_limit_bytes=VMEM_LIMIT_BYTES),
        name="merge_ffn",
    )(x, o, *consts)


def _rope_tables(seq, d_rot):
    d_ax = d_rot // 2
    inv = np.float32(ROPE_THETA) ** (-(np.arange(0, d_ax, 2, dtype=np.float32) / np.float32(d_ax)))
    rows = seq // GRID_W
    row = np.repeat(np.arange(rows, dtype=np.float32), GRID_W)
    col = np.tile(np.arange(GRID_W, dtype=np.float32), rows)
    ang_r, ang_c = row[:, None] * inv[None, :], col[:, None] * inv[None, :]
    c = np.concatenate([np.cos(ang_r)] * 2 + [np.cos(ang_c)] * 2, axis=-1)
    s = np.concatenate([-np.sin(ang_r), np.sin(ang_r), -np.sin(ang_c), np.sin(ang_c)], axis=-1)
    return c.astype(np.float32), s.astype(np.float32)


def _swap_halves(a):
    d_rot = a.shape[-1]
    a = a.reshape(a.shape[:-1] + (2, 2, d_rot // 4))
    return jnp.flip(a, axis=-2).reshape(a.shape[:-3] + (d_rot,))


def _row(v):
    return v.reshape(1, -1).astype(jnp.float32)


def _pad_lanes(v, front, width):
    return jnp.pad(v, [(0, 0)] * (v.ndim - 1) + [(front, width - front - v.shape[-1])])


def _gqa_slot_order(a, axis):
    shape = a.shape
    a = a.reshape(shape[:axis] + (GQA_KV_HEADS, GQA_GROUP, GQA_HD) + shape[axis + 1:])
    a = jnp.swapaxes(a, axis, axis + 1)
    return a.reshape(shape)


def kernel(x, norm1_g, w_in, q_a_norm_g, w_q_b, kv_a_norm_g, w_kv_b, mla_q_norm_g, mla_k_norm_g,
           gqa_q_norm_g, gqa_k_norm_g, mla_out_norm_g, gqa_out_norm_g, w_o, norm2_g, w_gate, w_up,
           w_down):
    B, S, _ = x.shape
    bf = jnp.bfloat16
    depth = w_in.shape[0]
    assert S % GRID_W == 0 and S % TB_PROJ == 0 and TB_PROJ % TS_PROJ == 0 and S % TM_FFN == 0
    assert S % TQ_BLOCK == 0 and TQ_BLOCK % TQ_ATTN == 0

    cm, sm = _rope_tables(S, MLA_ROPE)
    cg, sg = _rope_tables(S, GQA_HD)
    ncat = lambda *parts: jnp.asarray(np.concatenate(parts, axis=-1))
    tables = (ncat(np.ones((S, MLA_NOPE), np.float32), cm, sm),
              ncat(cm, sm, cm, sm),
              ncat(cg, sg), ncat(cg, cg), ncat(sg, sg))
    cat = lambda *parts: jnp.concatenate(parts, axis=-1)

    for l in range(depth):
        wi = w_in[l]
        o0 = Q_LORA + KV_LORA
        o1 = o0 + MLA_ROPE
        o2 = o1 + GQA_HEADS * GQA_HD
        o3 = o2 + GQA_KV_HEADS * GQA_HD
        w_kpe, w_gq = wi[:, o0:o1], wi[:, o1:o2].reshape(D_MODEL, GQA_HEADS, GQA_HD)
        w_gk = wi[:, o2:o3].reshape(D_MODEL, GQA_KV_HEADS, GQA_HD)
        win = cat(
            wi[:, :o0],
            w_kpe, _swap_halves(w_kpe), w_kpe, _swap_halves(w_kpe),
            cat(w_gq, _swap_halves(w_gq)).reshape(D_MODEL, GQA_HEADS * LANES),
            w_gk.reshape(D_MODEL, LANES), _swap_halves(w_gk).reshape(D_MODEL, LANES),
            wi[:, o3:]).astype(bf)
        wq = w_q_b[l].reshape(Q_LORA, MLA_HEADS, MLA_QK)
        wqb = cat(wq, _swap_halves(wq[..., MLA_NOPE:])).reshape(Q_LORA, MLA_HEADS * LANES).astype(bf)
        wkv = w_kv_b[l].reshape(KV_LORA, MLA_HEADS, MLA_NOPE + MLA_V)
        wkvb = cat(
            _pad_lanes(wkv[..., :MLA_NOPE], 0, LANES).reshape(KV_LORA, MLA_HEADS * LANES),
            wkv[..., MLA_NOPE:].reshape(KV_LORA, MLA_HEADS * MLA_V)).astype(bf)
        gmq, gmk = mla_q_norm_g[l], mla_k_norm_g[l]
        gmk_pe = gmk[MLA_NOPE:]
        ggq, ggk = gqa_q_norm_g[l], gqa_k_norm_g[l]
        proj_consts = (
            _row(norm1_g[l]), win, _row(q_a_norm_g[l]), wqb, _row(kv_a_norm_g[l]), wkvb,
            _row(cat(gmq, _swap_halves(gmq[MLA_NOPE:]))),
            _row(_pad_lanes(gmk[:MLA_NOPE], 0, LANES)),
            _row(cat(gmk_pe, _swap_halves(gmk_pe), gmk_pe, _swap_halves(gmk_pe))),
            _row(cat(ggq, _swap_halves(ggq))),
            _row(cat(ggk, ggk)), _row(cat(_swap_halves(ggk), _swap_halves(ggk))))
        q, k, v = _projections(x, proj_consts, tables)

        o = _attention(q, k, v)

        n_a = MLA_HEADS * MLA_V
        wo = jnp.concatenate([w_o[l][:n_a], _gqa_slot_order(w_o[l][n_a:], 0)], axis=0).astype(bf)
        ffn_consts = (
            _row(mla_out_norm_g[l]), _row(_gqa_slot_order(gqa_out_norm_g[l], 0)), wo,
            _row(norm2_g[l]), w_gate[l].astype(bf), w_up[l].astype(bf), w_down[l].astype(bf))
        x = _merge_ffn(x, o, ffn_consts)
    return x
```

```python
import math

import jax
import jax.numpy as jnp
import numpy as np
from jax import lax
from jax.experimental import pallas as pl
from jax.experimental.pallas import tpu as pltpu

D_MODEL = 1024
GRID_W = 64
ROPE_THETA = 10000.0
EPS = 1e-6

MLA_HEADS = 8
Q_LORA = 384
KV_LORA = 256
MLA_NOPE = 64
MLA_ROPE = 32
MLA_V = 64
MLA_QK = MLA_NOPE + MLA_ROPE

GQA_HEADS = 8
GQA_KV_HEADS = 2
GQA_HD = 64
GQA_GROUP = GQA_HEADS // GQA_KV_HEADS

D_FF = 2816

LANES = 128
V_HALF = 64
LOG2E = math.log2(math.e)
VMEM_LIMIT_BYTES = 56 * 1024 * 1024

TB_PROJ = 1024
TS_PROJ = 256
TQ_BLOCK = 1024
TQ_ATTN = 512
TM_FFN = 512

MLA_PAIRS = MLA_HEADS // 2
N_PAIRS = MLA_PAIRS + GQA_GROUP
Q_SLOTS = MLA_HEADS + GQA_HEADS
K_SLOTS = MLA_HEADS + GQA_KV_HEADS
V_SLOTS = MLA_PAIRS + 1

_P_CQ = (0, Q_LORA)
_P_CKV = (_P_CQ[1], _P_CQ[1] + KV_LORA)
_P_KPE = (_P_CKV[1], _P_CKV[1] + LANES)
_P_GQ = (_P_KPE[1], _P_KPE[1] + GQA_HEADS * LANES)
_P_GK = (_P_GQ[1], _P_GQ[1] + LANES)
_P_GK_P = (_P_GK[1], _P_GK[1] + LANES)
_P_GV = (_P_GK_P[1], _P_GK_P[1] + LANES)
P_WIDTH = _P_GV[1]

(_G_NORM1, _G_CQ, _G_CKV, _G_MLA_Q, _G_MLA_KN, _G_MLA_KP, _G_GQA_Q, _G_GQA_K, _G_GQA_KP) = range(9)
_G_OUT_A, _G_OUT_B, _G_NORM2 = range(3)
GAIN_ROWS = 16


def _bf16_dot(a, b):
    return jnp.dot(a.astype(jnp.bfloat16), b, preferred_element_type=jnp.float32)


def _rms_scale(x, width):
    return lax.rsqrt(jnp.sum(x * x, axis=-1, keepdims=True) * (1.0 / width) + EPS)


def _proj_kernel(x_ref, xn_ref, gain_ref, win_ref, wqb_ref, wkvb_ref,
                 tmq_ref, tmk_ref, tgq_ref, tgc_ref, tgs_ref,
                 q_ref, k_ref, v_ref, hb0, hb1, ql0, ql1, kv0, kv1, pr0, pr1):
    ts = ql0.shape[0]
    n_items = x_ref.shape[0] // ts
    assert n_items % 2 == 0
    hb, ql, kvs, pr = (hb0, hb1), (ql0, ql1), (kv0, kv1), (pr0, pr1)
    gain = lambda row, width: gain_ref[row:row + 1, :width]
    g1, gcq, gckv = gain(_G_NORM1, D_MODEL), gain(_G_CQ, Q_LORA), gain(_G_CKV, KV_LORA)
    gmq, gmkn, gmkp = gain(_G_MLA_Q, LANES), gain(_G_MLA_KN, LANES), gain(_G_MLA_KP, LANES)
    ggq, ggk, ggkp = gain(_G_GQA_Q, LANES), gain(_G_GQA_K, LANES), gain(_G_GQA_KP, LANES)

    def rows(it):
        return pl.ds(it * ts, ts)

    def stage_norm(src_ref, it, h_dst):
        x = src_ref[rows(it), :]
        h_dst[...] = (x * _rms_scale(x, D_MODEL) * g1).astype(h_dst.dtype)

    def stage_matmuls(h_src, ql_dst, kv_dst, pr_dst):
        p = jnp.dot(h_src[...], win_ref[...], preferred_element_type=jnp.float32)
        cq = p[:, _P_CQ[0]:_P_CQ[1]]
        ckv = p[:, _P_CKV[0]:_P_CKV[1]]
        ql_dst[...] = _bf16_dot(cq * _rms_scale(cq, Q_LORA) * gcq, wqb_ref[...])
        kv_dst[...] = _bf16_dot(ckv * _rms_scale(ckv, KV_LORA) * gckv, wkvb_ref[...])
        pr_dst[...] = p[:, _P_KPE[0]:]

    def stage_heads(it, ql_src, kv_src, pr_src):
        r = rows(it)
        off = lambda span: slice(span[0] - _P_KPE[0], span[1] - _P_KPE[0])
        lane = lax.broadcasted_iota(jnp.int32, (ts, LANES), 1)
        low = lane < LANES // 2

        y = pr_src[:, off(_P_KPE)] * gmkp * tmk_ref[r, :]
        kpe_rot = y + pltpu.roll(y, MLA_ROPE, 1)
        kpe = jnp.where(lane < MLA_ROPE, pr_src[:, off(_P_KPE)], 0.0)
        ss_pe = jnp.sum(kpe * kpe, axis=-1, keepdims=True)
        tmq = tmq_ref[r, :] * (gmq * (MLA_QK ** -0.5 * LOG2E))
        for hh in range(MLA_HEADS):
            sl = slice(hh * LANES, (hh + 1) * LANES)
            qh = ql_src[:, sl]
            qsq = jnp.where(lane < MLA_QK, qh * qh, 0.0)
            rq = lax.rsqrt(jnp.sum(qsq, axis=-1, keepdims=True) * (1.0 / MLA_QK) + EPS)
            q_ref[hh, r, :] = (qh * rq * tmq).astype(q_ref.dtype)
            kn = kv_src[:, sl]
            rk = lax.rsqrt((jnp.sum(kn * kn, axis=-1, keepdims=True) + ss_pe) * (1.0 / MLA_QK) + EPS)
            k_ref[hh, r, :] = (jnp.where(low, kn * gmkn, kpe_rot) * rk).astype(k_ref.dtype)
        for j in range(MLA_PAIRS):
            lo = MLA_HEADS * LANES + j * LANES
            v_ref[j, r, :] = kv_src[:, lo:lo + LANES].astype(v_ref.dtype)

        tgq = tgq_ref[r, :] * (ggq * (GQA_HD ** -0.5 * LOG2E))
        gq0 = off(_P_GQ).start
        for hh in range(GQA_HEADS):
            qh = pr_src[:, gq0 + hh * LANES:gq0 + (hh + 1) * LANES]
            q_ref[MLA_HEADS + hh, r, :] = (qh * _rms_scale(qh, LANES) * tgq).astype(q_ref.dtype)
        gk, gk_p = pr_src[:, off(_P_GK)], pr_src[:, off(_P_GK_P)]
        sq = gk * gk
        ss_lo = jnp.sum(jnp.where(low, sq, 0.0), axis=-1, keepdims=True)
        ss_hi = jnp.sum(jnp.where(low, 0.0, sq), axis=-1, keepdims=True)
        rk = jnp.where(low, lax.rsqrt(ss_lo * (1.0 / GQA_HD) + EPS), lax.rsqrt(ss_hi * (1.0 / GQA_HD) + EPS))
        rot = (gk * (ggk * tgc_ref[r, :]) + gk_p * (ggkp * tgs_ref[r, :])) * rk
        rot_x = pltpu.roll(rot, GQA_HD, 1)
        k_ref[MLA_HEADS, r, :] = jnp.where(low, rot, rot_x).astype(k_ref.dtype)
        k_ref[MLA_HEADS + 1, r, :] = jnp.where(low, rot_x, rot).astype(k_ref.dtype)
        v_ref[MLA_PAIRS, r, :] = pr_src[:, off(_P_GV)].astype(v_ref.dtype)

    @pl.when((pl.program_id(0) == 0) & (pl.program_id(1) == 0))
    def _():
        stage_norm(x_ref, 0, hb0)
        stage_norm(x_ref, 1, hb1)
        stage_matmuls(hb0, ql0, kv0, pr0)

    for it in range(n_items):
        cur, ahead = it % 2, it + 2
        src, idx = (x_ref, ahead) if ahead < n_items else (xn_ref, ahead - n_items)
        stage_norm(src, idx, hb[cur])
        stage_matmuls(hb[1 - cur], ql[1 - cur], kvs[1 - cur], pr[1 - cur])
        stage_heads(it, ql[cur], kvs[cur], pr[cur])


def _const_spec(shape):
    return pl.BlockSpec(shape, lambda *_: (0,) * len(shape), pipeline_mode=pl.Buffered(1))


def _projections(x, consts, tables):
    B, S, _ = x.shape
    tb, ts = TB_PROJ, TS_PROJ
    slots = lambda n: pl.BlockSpec((None, n, tb, LANES), lambda b, i: (b, 0, i, 0))
    tab = pl.BlockSpec((tb, LANES), lambda b, i: (i, 0))
    n_slots = (Q_SLOTS, K_SLOTS, V_SLOTS)
    widths = (MLA_HEADS * LANES, MLA_HEADS * (LANES + MLA_V), P_WIDTH - _P_KPE[0])
    n_i = S // tb

    def next_block(b, i):
        t = jnp.minimum(b * n_i + i + 1, B * n_i - 1)
        return t // n_i, t % n_i, 0

    return pl.pallas_call(
        _proj_kernel,
        grid=(B, n_i),
        in_specs=[pl.BlockSpec((None, tb, D_MODEL), lambda b, i: (b, i, 0)),
                  pl.BlockSpec((None, tb, D_MODEL), next_block)]
                 + [_const_spec(c.shape) for c in consts] + [tab] * len(tables),
        out_specs=[slots(n) for n in n_slots],
        out_shape=[jax.ShapeDtypeStruct((B, n, S, LANES), jnp.bfloat16) for n in n_slots],
        scratch_shapes=[pltpu.VMEM((ts, D_MODEL), jnp.bfloat16)] * 2
                       + [pltpu.VMEM((ts, w), jnp.float32) for w in widths for _ in range(2)],
        compiler_params=pltpu.CompilerParams(
            dimension_semantics=("arbitrary", "arbitrary"), vmem_limit_bytes=VMEM_LIMIT_BYTES),
        name="projections",
    )(x, x, *consts, *tables)


def _attn_kernel(q_ref, k_ref, v_ref, o_ref, sc0, sc1, mx0, mx1):
    tq, kv_len = sc0.shape[1:]
    n_items = (q_ref.shape[1] // tq) * N_PAIRS
    sc, mx = (sc0, sc1), (mx0, mx1)
    v_low = lax.broadcasted_iota(jnp.int32, (kv_len, LANES), 1) < V_HALF
    o_low = lax.broadcasted_iota(jnp.int32, (tq, LANES), 1) < V_HALF
    one = jnp.ones((), v_ref.dtype)

    def item(it):
        it = jnp.minimum(it, n_items - 1)
        return pl.multiple_of((it // N_PAIRS) * tq, tq), it % N_PAIRS

    def stage_scores(it, s_dst, m_dst):
        row0, j = item(it)
        mla = j < MLA_PAIRS
        for e in range(2):
            qi = jnp.where(mla, 2 * j + e, MLA_HEADS + j - MLA_PAIRS + e * GQA_GROUP)
            ki = jnp.where(mla, 2 * j + e, MLA_HEADS + e)
            s = lax.dot_general(q_ref[qi, pl.ds(row0, tq), :], k_ref[ki],
                                (((1,), (1,)), ((), ())), preferred_element_type=jnp.float32)
            s_dst[e] = s
            m_dst[e] = jnp.broadcast_to(jnp.max(s, axis=-1, keepdims=True), (tq, LANES))

    def stage_values(it, s_src, m_src):
        row0, j = item(it)
        vi = jnp.minimum(j, MLA_PAIRS)
        v = v_ref[vi]
        accs = []
        for e in range(2):
            m = m_src[e]
            p = jnp.concatenate([jnp.exp2(s_src[e, :, c:c + LANES] - m).astype(v.dtype)
                                 for c in range(0, kv_len, LANES)], axis=-1)
            vh = jnp.where(v_low, v, one) if e == 0 else jnp.where(v_low, one, v)
            accs.append(jnp.dot(p, vh, preferred_element_type=jnp.float32))
        num = jnp.where(o_low, accs[0], accs[1])
        den = jnp.where(o_low, pltpu.roll(accs[0], V_HALF, 1), pltpu.roll(accs[1], V_HALF, 1))
        o_ref[j, pl.ds(row0, tq), :] = (num / den).astype(o_ref.dtype)

    stage_scores(0, sc0, mx0)
    stage_scores(1, sc1, mx1)

    def trip(t, carry):
        stage_values(2 * t, sc0, mx0)
        stage_values(2 * t + 1, sc1, mx1)
        stage_scores(2 * t + 2, sc0, mx0)
        stage_scores(2 * t + 3, sc1, mx1)
        return carry

    lax.fori_loop(0, n_items // 2 - 1, trip, 0)
    stage_values(n_items - 2, sc0, mx0)
    stage_values(n_items - 1, sc1, mx1)


def _attention(q, k, v):
    B, _, S, _ = q.shape
    tb, tq = TQ_BLOCK, TQ_ATTN
    pair_buf = lambda w: pltpu.VMEM((2, tq, w), jnp.float32)
    return pl.pallas_call(
        _attn_kernel,
        grid=(B, S // tb),
        in_specs=[pl.BlockSpec((None, Q_SLOTS, tb, LANES), lambda b, i: (b, 0, i, 0)),
                  pl.BlockSpec((None, K_SLOTS, S, LANES), lambda b, i: (b, 0, 0, 0)),
                  pl.BlockSpec((None, V_SLOTS, S, LANES), lambda b, i: (b, 0, 0, 0))],
        out_specs=pl.BlockSpec((None, N_PAIRS, tb, LANES), lambda b, i: (b, 0, i, 0)),
        out_shape=jax.ShapeDtypeStruct((B, N_PAIRS, S, LANES), jnp.bfloat16),
        scratch_shapes=[pair_buf(S), pair_buf(S), pair_buf(LANES), pair_buf(LANES)],
        compiler_params=pltpu.CompilerParams(
            dimension_semantics=("parallel", "parallel"), vmem_limit_bytes=VMEM_LIMIT_BYTES),
        name="attention",
    )(q, k, v)


def _ffn_kernel(x_ref, o_ref, gain_ref, wo_ref, wg_ref, wu_ref, wd_ref, out_ref):
    oa = jnp.concatenate([o_ref[j] for j in range(MLA_PAIRS)], axis=-1).astype(jnp.float32)
    ob = jnp.concatenate([o_ref[j] for j in range(MLA_PAIRS, N_PAIRS)], axis=-1).astype(jnp.float32)
    na = oa * _rms_scale(oa, oa.shape[-1]) * gain_ref[_G_OUT_A:_G_OUT_A + 1, :oa.shape[-1]]
    nb = ob * _rms_scale(ob, ob.shape[-1]) * gain_ref[_G_OUT_B:_G_OUT_B + 1, :ob.shape[-1]]
    mixed = jnp.concatenate([na, nb], axis=-1)
    x1 = x_ref[...] + _bf16_dot(mixed, wo_ref[...])
    h2 = (x1 * _rms_scale(x1, D_MODEL) * gain_ref[_G_NORM2:_G_NORM2 + 1, :]).astype(jnp.bfloat16)
    g = jnp.dot(h2, wg_ref[...], preferred_element_type=jnp.float32)
    u = jnp.dot(h2, wu_ref[...], preferred_element_type=jnp.float32)
    silu = g / (1.0 + jnp.exp(-g))
    out_ref[...] = x1 + _bf16_dot(silu * u, wd_ref[...])


def _merge_ffn(x, o, consts):
    B, S, _ = x.shape
    tm = TM_FFN
    tok = pl.BlockSpec((None, tm, D_MODEL), lambda b, i: (b, i, 0))
    return pl.pallas_call(
        _ffn_kernel,
        grid=(B, S // tm),
        in_specs=[tok, pl.BlockSpec((None, N_PAIRS, tm, LANES), lambda b, i: (b, 0, i, 0))]
                 + [_const_spec(c.shape) for c in consts],
        out_specs=tok,
        out_shape=jax.ShapeDtypeStruct(x.shape, x.dtype),
        compiler_params=pltpu.CompilerParams(
            dimension_semantics=("parallel", "parallel"), vmem_limit_bytes=VMEM_LIMIT_BYTES),
        name="merge_ffn",
    )(x, o, *consts)


def _rope_tables(seq, d_rot):
    d_ax = d_rot // 2
    inv = np.float32(ROPE_THETA) ** (-(np.arange(0, d_ax, 2, dtype=np.float32) / np.float32(d_ax)))
    rows = seq // GRID_W
    row = np.repeat(np.arange(rows, dtype=np.float32), GRID_W)
    col = np.tile(np.arange(GRID_W, dtype=np.float32), rows)
    ang_r, ang_c = row[:, None] * inv[None, :], col[:, None] * inv[None, :]
    c = np.concatenate([np.cos(ang_r)] * 2 + [np.cos(ang_c)] * 2, axis=-1)
    s = np.concatenate([-np.sin(ang_r), np.sin(ang_r), -np.sin(ang_c), np.sin(ang_c)], axis=-1)
    return c.astype(np.float32), s.astype(np.float32)


def _swap_halves(a):
    d_rot = a.shape[-1]
    a = a.reshape(a.shape[:-1] + (2, 2, d_rot // 4))
    return a[..., ::-1, :].reshape(a.shape[:-3] + (d_rot,))


def _gain_rows(params, make_rows):
    sizes = [p.shape[0] for p in params]
    starts = np.cumsum([0] + sizes)
    positions = [np.arange(a, a + n) for a, n in zip(starts, sizes)]
    table = np.full((GAIN_ROWS, D_MODEL), starts[-1], np.int32)
    for r, ix in enumerate(make_rows(*positions)):
        table[r, :ix.shape[0]] = ix
    flat = jnp.concatenate([p.astype(jnp.float32) for p in params] + [jnp.zeros((1,), jnp.float32)])
    return flat[table]


def _pad_lanes(v, front, width):
    return jnp.pad(v, [(0, 0)] * (v.ndim - 1) + [(front, width - front - v.shape[-1])])


def _gqa_slot_order(a, axis):
    shape = a.shape
    a = a.reshape(shape[:axis] + (GQA_KV_HEADS, GQA_GROUP, GQA_HD) + shape[axis + 1:])
    return a.swapaxes(axis, axis + 1).reshape(shape)


def kernel(x, norm1_g, w_in, q_a_norm_g, w_q_b, kv_a_norm_g, w_kv_b, mla_q_norm_g, mla_k_norm_g,
           gqa_q_norm_g, gqa_k_norm_g, mla_out_norm_g, gqa_out_norm_g, w_o, norm2_g, w_gate, w_up,
           w_down):
    B, S, _ = x.shape
    bf = jnp.bfloat16
    depth = w_in.shape[0]
    assert S % GRID_W == 0 and S % TB_PROJ == 0 and TB_PROJ % TS_PROJ == 0 and S % TM_FFN == 0
    assert S % TQ_BLOCK == 0 and TQ_BLOCK % TQ_ATTN == 0

    cm, sm = _rope_tables(S, MLA_ROPE)
    cg, sg = _rope_tables(S, GQA_HD)
    ncat = lambda *parts: jnp.asarray(np.concatenate(parts, axis=-1))
    tables = (ncat(np.ones((S, MLA_NOPE), np.float32), cm, sm),
              ncat(cm, sm, cm, sm),
              ncat(cg, sg), ncat(cg, cg), ncat(sg, sg))
    cat = lambda *parts: jnp.concatenate(parts, axis=-1)

    for l in range(depth):
        wi = w_in[l]
        o0 = Q_LORA + KV_LORA
        o1 = o0 + MLA_ROPE
        o2 = o1 + GQA_HEADS * GQA_HD
        o3 = o2 + GQA_KV_HEADS * GQA_HD
        w_kpe, w_gq = wi[:, o0:o1], wi[:, o1:o2].reshape(D_MODEL, GQA_HEADS, GQA_HD)
        w_gk = wi[:, o2:o3].reshape(D_MODEL, GQA_KV_HEADS, GQA_HD)
        win = cat(
            wi[:, :o0],
            w_kpe, _swap_halves(w_kpe), w_kpe, _swap_halves(w_kpe),
            cat(w_gq, _swap_halves(w_gq)).reshape(D_MODEL, GQA_HEADS * LANES),
            w_gk.reshape(D_MODEL, LANES), _swap_halves(w_gk).reshape(D_MODEL, LANES),
            wi[:, o3:]).astype(bf)
        wq = w_q_b[l].reshape(Q_LORA, MLA_HEADS, MLA_QK)
        wqb = cat(wq, _swap_halves(wq[..., MLA_NOPE:])).reshape(Q_LORA, MLA_HEADS * LANES).astype(bf)
        wkv = w_kv_b[l].reshape(KV_LORA, MLA_HEADS, MLA_NOPE + MLA_V)
        wkvb = cat(
            _pad_lanes(wkv[..., :MLA_NOPE], 0, LANES).reshape(KV_LORA, MLA_HEADS * LANES),
            wkv[..., MLA_NOPE:].reshape(KV_LORA, MLA_HEADS * MLA_V)).astype(bf)

        def proj_rows(g1, gcq, gckv, gmq, gmk, ggq, ggk):
            gmk_pe, pos = gmk[MLA_NOPE:], np.concatenate
            return (g1, gcq, gckv,
                    pos([gmq, _swap_halves(gmq[MLA_NOPE:])]),
                    gmk[:MLA_NOPE],
                    pos([gmk_pe, _swap_halves(gmk_pe), gmk_pe, _swap_halves(gmk_pe)]),
                    pos([ggq, _swap_halves(ggq)]),
                    pos([ggk, ggk]), pos([_swap_halves(ggk), _swap_halves(ggk)]))

        proj_gains = _gain_rows((norm1_g[l], q_a_norm_g[l], kv_a_norm_g[l], mla_q_norm_g[l],
                                 mla_k_norm_g[l], gqa_q_norm_g[l], gqa_k_norm_g[l]), proj_rows)
        proj_consts = (proj_gains, win, wqb, wkvb)
        q, k, v = _projections(x, proj_consts, tables)

        o = _attention(q, k, v)

        n_a = MLA_HEADS * MLA_V
        wo = jnp.concatenate([w_o[l][:n_a], _gqa_slot_order(w_o[l][n_a:], 0)], axis=0).astype(bf)
        ffn_gains = _gain_rows((mla_out_norm_g[l], gqa_out_norm_g[l], norm2_g[l]),
                               lambda ga, gb, g2: (ga, _gqa_slot_order(gb, 0), g2))
        ffn_consts = (ffn_gains, wo, w_gate[l].astype(bf), w_up[l].astype(bf), w_down[l].astype(bf))
        x = _merge_ffn(x, o, ffn_consts)
    return x
```

```python
import math

import jax
import jax.numpy as jnp
import numpy as np
from jax import lax
from jax.experimental import pallas as pl
from jax.experimental.pallas import tpu as pltpu

D_MODEL = 1024
GRID_W = 64
ROPE_THETA = 10000.0
EPS = 1e-6

MLA_HEADS = 8
Q_LORA = 384
KV_LORA = 256
MLA_NOPE = 64
MLA_ROPE = 32
MLA_V = 64
MLA_QK = MLA_NOPE + MLA_ROPE

GQA_HEADS = 8
GQA_KV_HEADS = 2
GQA_HD = 64
GQA_GROUP = GQA_HEADS // GQA_KV_HEADS

D_FF = 2816

LANES = 128
V_HALF = 64
LOG2E = math.log2(math.e)
VMEM_LIMIT_BYTES = 56 * 1024 * 1024

TB_PROJ = 1024
TS_PROJ = 256
TQ_BLOCK = 1024
TQ_ATTN = 512
TM_FFN = 512

MLA_PAIRS = MLA_HEADS // 2
N_PAIRS = MLA_PAIRS + GQA_GROUP
Q_SLOTS = MLA_HEADS + GQA_HEADS
K_SLOTS = MLA_HEADS + GQA_KV_HEADS
V_SLOTS = MLA_PAIRS + 1

_P_CQ = (0, Q_LORA)
_P_CKV = (_P_CQ[1], _P_CQ[1] + KV_LORA)
_P_KPE = (_P_CKV[1], _P_CKV[1] + LANES)
_P_GQ = (_P_KPE[1], _P_KPE[1] + GQA_HEADS * LANES)
_P_GK = (_P_GQ[1], _P_GQ[1] + LANES)
_P_GK_P = (_P_GK[1], _P_GK[1] + LANES)
_P_GV = (_P_GK_P[1], _P_GK_P[1] + LANES)
P_WIDTH = _P_GV[1]

(_G_NORM1, _G_CQ, _G_CKV, _G_MLA_Q, _G_MLA_KN, _G_MLA_KP, _G_GQA_Q, _G_GQA_K, _G_GQA_KP) = range(9)
_G_OUT_A, _G_OUT_B, _G_NORM2 = range(3)
GAIN_ROWS = 16


def _bf16_dot(a, b):
    return jnp.dot(a.astype(jnp.bfloat16), b, preferred_element_type=jnp.float32)


def _rms_scale(x, width):
    return lax.rsqrt(jnp.sum(x * x, axis=-1, keepdims=True) * (1.0 / width) + EPS)


def _proj_kernel(x_ref, xn_ref, gain_ref, win_ref, wqb_ref, wkvb_ref,
                 tmq_ref, tmk_ref, tgq_ref, tgc_ref, tgs_ref,
                 q_ref, k_ref, v_ref, hb0, hb1, ql0, ql1, kv0, kv1, pr0, pr1):
    ts = ql0.shape[0]
    n_items = x_ref.shape[0] // ts
    assert n_items % 2 == 0
    hb, ql, kvs, pr = (hb0, hb1), (ql0, ql1), (kv0, kv1), (pr0, pr1)
    gain = lambda row, width: gain_ref[row:row + 1, :width]
    g1, gcq, gckv = gain(_G_NORM1, D_MODEL), gain(_G_CQ, Q_LORA), gain(_G_CKV, KV_LORA)
    gmq, gmkn, gmkp = gain(_G_MLA_Q, LANES), gain(_G_MLA_KN, LANES), gain(_G_MLA_KP, LANES)
    ggq, ggk, ggkp = gain(_G_GQA_Q, LANES), gain(_G_GQA_K, LANES), gain(_G_GQA_KP, LANES)

    def rows(it):
        return pl.ds(it * ts, ts)

    def stage_norm(src_ref, it, h_dst):
        x = src_ref[rows(it), :]
        h_dst[...] = (x * _rms_scale(x, D_MODEL) * g1).astype(h_dst.dtype)

    def stage_matmuls(h_src, ql_dst, kv_dst, pr_dst):
        p = jnp.dot(h_src[...], win_ref[...], preferred_element_type=jnp.float32)
        cq = p[:, _P_CQ[0]:_P_CQ[1]]
        ckv = p[:, _P_CKV[0]:_P_CKV[1]]
        ql_dst[...] = _bf16_dot(cq * _rms_scale(cq, Q_LORA) * gcq, wqb_ref[...])
        kv_dst[...] = _bf16_dot(ckv * _rms_scale(ckv, KV_LORA) * gckv, wkvb_ref[...])
        pr_dst[...] = p[:, _P_KPE[0]:]

    def stage_heads(it, ql_src, kv_src, pr_src):
        r = rows(it)
        off = lambda span: slice(span[0] - _P_KPE[0], span[1] - _P_KPE[0])
        lane = lax.broadcasted_iota(jnp.int32, (ts, LANES), 1)
        low = lane < LANES // 2

        y = pr_src[:, off(_P_KPE)] * gmkp * tmk_ref[r, :]
        kpe_rot = y + pltpu.roll(y, MLA_ROPE, 1)
        kpe = jnp.where(lane < MLA_ROPE, pr_src[:, off(_P_KPE)], 0.0)
        ss_pe = jnp.sum(kpe * kpe, axis=-1, keepdims=True)
        tmq = tmq_ref[r, :] * (gmq * (MLA_QK ** -0.5 * LOG2E))
        for hh in range(MLA_HEADS):
            sl = slice(hh * LANES, (hh + 1) * LANES)
            qh = ql_src[:, sl]
            qsq = jnp.where(lane < MLA_QK, qh * qh, 0.0)
            rq = lax.rsqrt(jnp.sum(qsq, axis=-1, keepdims=True) * (1.0 / MLA_QK) + EPS)
            q_ref[hh, r, :] = (qh * rq * tmq).astype(q_ref.dtype)
            kn = kv_src[:, sl]
            rk = lax.rsqrt((jnp.sum(kn * kn, axis=-1, keepdims=True) + ss_pe) * (1.0 / MLA_QK) + EPS)
            k_ref[hh, r, :] = (jnp.where(low, kn * gmkn, kpe_rot) * rk).astype(k_ref.dtype)
        for j in range(MLA_PAIRS):
            lo = MLA_HEADS * LANES + j * LANES
            v_ref[j, r, :] = kv_src[:, lo:lo + LANES].astype(v_ref.dtype)

        tgq = tgq_ref[r, :] * (ggq * (GQA_HD ** -0.5 * LOG2E))
        gq0 = off(_P_GQ).start
        for hh in range(GQA_HEADS):
            qh = pr_src[:, gq0 + hh * LANES:gq0 + (hh + 1) * LANES]
            q_ref[MLA_HEADS + hh, r, :] = (qh * _rms_scale(qh, LANES) * tgq).astype(q_ref.dtype)
        gk, gk_p = pr_src[:, off(_P_GK)], pr_src[:, off(_P_GK_P)]
        sq = gk * gk
        ss_lo = jnp.sum(jnp.where(low, sq, 0.0), axis=-1, keepdims=True)
        ss_hi = jnp.sum(jnp.where(low, 0.0, sq), axis=-1, keepdims=True)
        rk = jnp.where(low, lax.rsqrt(ss_lo * (1.0 / GQA_HD) + EPS), lax.rsqrt(ss_hi * (1.0 / GQA_HD) + EPS))
        rot = (gk * (ggk * tgc_ref[r, :]) + gk_p * (ggkp * tgs_ref[r, :])) * rk
        rot_x = pltpu.roll(rot, GQA_HD, 1)
        k_ref[MLA_HEADS, r, :] = jnp.where(low, rot, rot_x).astype(k_ref.dtype)
        k_ref[MLA_HEADS + 1, r, :] = jnp.where(low, rot_x, rot).astype(k_ref.dtype)
        v_ref[MLA_PAIRS, r, :] = pr_src[:, off(_P_GV)].astype(v_ref.dtype)

    @pl.when((pl.program_id(0) == 0) & (pl.program_id(1) == 0))
    def _():
        stage_norm(x_ref, 0, hb0)
        stage_norm(x_ref, 1, hb1)
        stage_matmuls(hb0, ql0, kv0, pr0)

    for it in range(n_items):
        cur, ahead = it % 2, it + 2
        src, idx = (x_ref, ahead) if ahead < n_items else (xn_ref, ahead - n_items)
        stage_norm(src, idx, hb[cur])
        stage_matmuls(hb[1 - cur], ql[1 - cur], kvs[1 - cur], pr[1 - cur])
        stage_heads(it, ql[cur], kvs[cur], pr[cur])


def _const_spec(shape):
    return pl.BlockSpec(shape, lambda *_: (0,) * len(shape), pipeline_mode=pl.Buffered(1))


def _projections(x, consts, tables):
    B, S, _ = x.shape
    tb, ts = TB_PROJ, TS_PROJ
    slots = lambda n: pl.BlockSpec((None, n, tb, LANES), lambda b, i: (b, 0, i, 0))
    tab = pl.BlockSpec((tb, LANES), lambda b, i: (i, 0))
    n_slots = (Q_SLOTS, K_SLOTS, V_SLOTS)
    widths = (MLA_HEADS * LANES, MLA_HEADS * (LANES + MLA_V), P_WIDTH - _P_KPE[0])
    n_i = S // tb

    def next_block(b, i):
        t = jnp.minimum(b * n_i + i + 1, B * n_i - 1)
        return t // n_i, t % n_i, 0

    return pl.pallas_call(
        _proj_kernel,
        grid=(B, n_i),
        in_specs=[pl.BlockSpec((None, tb, D_MODEL), lambda b, i: (b, i, 0)),
                  pl.BlockSpec((None, tb, D_MODEL), next_block)]
                 + [_const_spec(c.shape) for c in consts] + [tab] * len(tables),
        out_specs=[slots(n) for n in n_slots],
        out_shape=[jax.ShapeDtypeStruct((B, n, S, LANES), jnp.bfloat16) for n in n_slots],
        scratch_shapes=[pltpu.VMEM((ts, D_MODEL), jnp.bfloat16)] * 2
                       + [pltpu.VMEM((ts, w), jnp.float32) for w in widths for _ in range(2)],
        compiler_params=pltpu.CompilerParams(
            dimension_semantics=("arbitrary", "arbitrary"), vmem_limit_bytes=VMEM_LIMIT_BYTES),
        name="projections",
    )(x, x, *consts, *tables)


def _attn_kernel(q_ref, k_ref, v_ref, o_ref, sc0, sc1, mx0, mx1):
    tq, kv_len = sc0.shape[1:]
    n_items = (q_ref.shape[1] // tq) * N_PAIRS
    sc, mx = (sc0, sc1), (mx0, mx1)
    v_low = lax.broadcasted_iota(jnp.int32, (kv_len, LANES), 1) < V_HALF
    o_low = lax.broadcasted_iota(jnp.int32, (tq, LANES), 1) < V_HALF
    one = jnp.ones((), v_ref.dtype)

    def item(it):
        it = jnp.minimum(it, n_items - 1)
        return pl.multiple_of((it // N_PAIRS) * tq, tq), it % N_PAIRS

    def stage_scores(it, s_dst, m_dst):
        row0, j = item(it)
        mla = j < MLA_PAIRS
        for e in range(2):
            qi = jnp.where(mla, 2 * j + e, MLA_HEADS + j - MLA_PAIRS + e * GQA_GROUP)
            ki = jnp.where(mla, 2 * j + e, MLA_HEADS + e)
            s = lax.dot_general(q_ref[qi, pl.ds(row0, tq), :], k_ref[ki],
                                (((1,), (1,)), ((), ())), preferred_element_type=jnp.float32)
            s_dst[e] = s
            m_dst[e] = jnp.broadcast_to(jnp.max(s, axis=-1, keepdims=True), (tq, LANES))

    def stage_values(it, s_src, m_src):
        row0, j = item(it)
        vi = jnp.minimum(j, MLA_PAIRS)
        v = v_ref[vi]
        accs = []
        for e in range(2):
            m = m_src[e]
            p = jnp.concatenate([jnp.exp2(s_src[e, :, c:c + LANES] - m).astype(v.dtype)
                                 for c in range(0, kv_len, LANES)], axis=-1)
            vh = jnp.where(v_low, v, one) if e == 0 else jnp.where(v_low, one, v)
            accs.append(jnp.dot(p, vh, preferred_element_type=jnp.float32))
        num = jnp.where(o_low, accs[0], accs[1])
        den = jnp.where(o_low, pltpu.roll(accs[0], V_HALF, 1), pltpu.roll(accs[1], V_HALF, 1))
        o_ref[j, pl.ds(row0, tq), :] = (num / den).astype(o_ref.dtype)

    stage_scores(0, sc0, mx0)
    stage_scores(1, sc1, mx1)

    def trip(t, carry):
        stage_values(2 * t, sc0, mx0)
        stage_values(2 * t + 1, sc1, mx1)
        stage_scores(2 * t + 2, sc0, mx0)
        stage_scores(2 * t + 3, sc1, mx1)
        return carry

    lax.fori_loop(0, n_items // 2 - 1, trip, 0)
    stage_values(n_items - 2, sc0, mx0)
    stage_values(n_items - 1, sc1, mx1)


def _attention(q, k, v):
    B, _, S, _ = q.shape
    tb, tq = TQ_BLOCK, TQ_ATTN
    pair_buf = lambda w: pltpu.VMEM((2, tq, w), jnp.float32)
    return pl.pallas_call(
        _attn_kernel,
        grid=(B, S // tb),
        in_specs=[pl.BlockSpec((None, Q_SLOTS, tb, LANES), lambda b, i: (b, 0, i, 0)),
                  pl.BlockSpec((None, K_SLOTS, S, LANES), lambda b, i: (b, 0, 0, 0)),
                  pl.BlockSpec((None, V_SLOTS, S, LANES), lambda b, i: (b, 0, 0, 0))],
        out_specs=pl.BlockSpec((None, N_PAIRS, tb, LANES), lambda b, i: (b, 0, i, 0)),
        out_shape=jax.ShapeDtypeStruct((B, N_PAIRS, S, LANES), jnp.bfloat16),
        scratch_shapes=[pair_buf(S), pair_buf(S), pair_buf(LANES), pair_buf(LANES)],
        compiler_params=pltpu.CompilerParams(
            dimension_semantics=("parallel", "parallel"), vmem_limit_bytes=VMEM_LIMIT_BYTES),
        name="attention",
    )(q, k, v)


def _ffn_kernel(x_ref, o_ref, gain_ref, wo_ref, wg_ref, wu_ref, wd_ref, out_ref):
    oa = jnp.concatenate([o_ref[j] for j in range(MLA_PAIRS)], axis=-1).astype(jnp.float32)
    ob = jnp.concatenate([o_ref[j] for j in range(MLA_PAIRS, N_PAIRS)], axis=-1).astype(jnp.float32)
    na = oa * _rms_scale(oa, oa.shape[-1]) * gain_ref[_G_OUT_A:_G_OUT_A + 1, :oa.shape[-1]]
    nb = ob * _rms_scale(ob, ob.shape[-1]) * gain_ref[_G_OUT_B:_G_OUT_B + 1, :ob.shape[-1]]
    mixed = jnp.concatenate([na, nb], axis=-1)
    x1 = x_ref[...] + _bf16_dot(mixed, wo_ref[...])
    h2 = (x1 * _rms_scale(x1, D_MODEL) * gain_ref[_G_NORM2:_G_NORM2 + 1, :]).astype(jnp.bfloat16)
    g = jnp.dot(h2, wg_ref[...], preferred_element_type=jnp.float32)
    u = jnp.dot(h2, wu_ref[...], preferred_element_type=jnp.float32)
    silu = g / (1.0 + jnp.exp(-g))
    out_ref[...] = x1 + _bf16_dot(silu * u, wd_ref[...])


def _merge_ffn(x, o, consts):
    B, S, _ = x.shape
    tm = TM_FFN
    tok = pl.BlockSpec((None, tm, D_MODEL), lambda b, i: (b, i, 0))
    return pl.pallas_call(
        _ffn_kernel,
        grid=(B, S // tm),
        in_specs=[tok, pl.BlockSpec((None, N_PAIRS, tm, LANES), lambda b, i: (b, 0, i, 0))]
                 + [_const_spec(c.shape) for c in consts],
        out_specs=tok,
        out_shape=jax.ShapeDtypeStruct(x.shape, x.dtype),
        compiler_params=pltpu.CompilerParams(
            dimension_semantics=("parallel", "parallel"), vmem_limit_bytes=VMEM_LIMIT_BYTES),
        name="merge_ffn",
    )(x, o, *consts)


def _rope_tables(seq, d_rot):
    d_ax = d_rot // 2
    inv = np.float32(ROPE_THETA) ** (-(np.arange(0, d_ax, 2, dtype=np.float32) / np.float32(d_ax)))
    rows = seq // GRID_W
    row = np.repeat(np.arange(rows, dtype=np.float32), GRID_W)
    col = np.tile(np.arange(GRID_W, dtype=np.float32), rows)
    ang_r, ang_c = row[:, None] * inv[None, :], col[:, None] * inv[None, :]
    c = np.concatenate([np.cos(ang_r)] * 2 + [np.cos(ang_c)] * 2, axis=-1)
    s = np.concatenate([-np.sin(ang_r), np.sin(ang_r), -np.sin(ang_c), np.sin(ang_c)], axis=-1)
    return c.astype(np.float32), s.astype(np.float32)


def _swap_halves(a):
    n = a.shape[-1] // 4
    return jnp.concatenate([a[..., n:2 * n], a[..., :n], a[..., 3 * n:], a[..., 2 * n:3 * n]], axis=-1)


def _gain_rows(rows):
    rows = [jnp.pad(r.astype(jnp.float32), (0, D_MODEL - r.shape[0])) for r in rows]
    return jnp.pad(jnp.stack(rows), ((0, GAIN_ROWS - len(rows)), (0, 0)))


def _pad_lanes(v, front, width):
    return jnp.pad(v, [(0, 0)] * (v.ndim - 1) + [(front, width - front - v.shape[-1])])


def _gqa_slot_order(a, axis):
    shape = a.shape
    a = a.reshape(shape[:axis] + (GQA_KV_HEADS, GQA_GROUP, GQA_HD) + shape[axis + 1:])
    return a.swapaxes(axis, axis + 1).reshape(shape)


def kernel(x, norm1_g, w_in, q_a_norm_g, w_q_b, kv_a_norm_g, w_kv_b, mla_q_norm_g, mla_k_norm_g,
           gqa_q_norm_g, gqa_k_norm_g, mla_out_norm_g, gqa_out_norm_g, w_o, norm2_g, w_gate, w_up,
           w_down):
    B, S, _ = x.shape
    bf = jnp.bfloat16
    depth = w_in.shape[0]
    assert S % GRID_W == 0 and S % TB_PROJ == 0 and TB_PROJ % TS_PROJ == 0 and S % TM_FFN == 0
    assert S % TQ_BLOCK == 0 and TQ_BLOCK % TQ_ATTN == 0

    cm, sm = _rope_tables(S, MLA_ROPE)
    cg, sg = _rope_tables(S, GQA_HD)
    ncat = lambda *parts: jnp.asarray(np.concatenate(parts, axis=-1))
    tables = (ncat(np.ones((S, MLA_NOPE), np.float32), cm, sm),
              ncat(cm, sm, cm, sm),
              ncat(cg, sg), ncat(cg, cg), ncat(sg, sg))
    cat = lambda *parts: jnp.concatenate(parts, axis=-1)

    for l in range(depth):
        wi = w_in[l]
        o0 = Q_LORA + KV_LORA
        o1 = o0 + MLA_ROPE
        o2 = o1 + GQA_HEADS * GQA_HD
        o3 = o2 + GQA_KV_HEADS * GQA_HD
        w_kpe, w_gq = wi[:, o0:o1], wi[:, o1:o2].reshape(D_MODEL, GQA_HEADS, GQA_HD)
        w_gk = wi[:, o2:o3].reshape(D_MODEL, GQA_KV_HEADS, GQA_HD)
        win = cat(
            wi[:, :o0],
            w_kpe, _swap_halves(w_kpe), w_kpe, _swap_halves(w_kpe),
            cat(w_gq, _swap_halves(w_gq)).reshape(D_MODEL, GQA_HEADS * LANES),
            w_gk.reshape(D_MODEL, LANES), _swap_halves(w_gk).reshape(D_MODEL, LANES),
            wi[:, o3:]).astype(bf)
        wq = w_q_b[l].reshape(Q_LORA, MLA_HEADS, MLA_QK)
        wqb = cat(wq, _swap_halves(wq[..., MLA_NOPE:])).reshape(Q_LORA, MLA_HEADS * LANES).astype(bf)
        wkv = w_kv_b[l].reshape(KV_LORA, MLA_HEADS, MLA_NOPE + MLA_V)
        wkvb = cat(
            _pad_lanes(wkv[..., :MLA_NOPE], 0, LANES).reshape(KV_LORA, MLA_HEADS * LANES),
            wkv[..., MLA_NOPE:].reshape(KV_LORA, MLA_HEADS * MLA_V)).astype(bf)
        gmq, gmk, ggq, ggk = mla_q_norm_g[l], mla_k_norm_g[l], gqa_q_norm_g[l], gqa_k_norm_g[l]
        gmk_pe, gmk_pe_p, ggk_p = gmk[MLA_NOPE:], _swap_halves(gmk[MLA_NOPE:]), _swap_halves(ggk)
        proj_gains = _gain_rows((
            norm1_g[l], q_a_norm_g[l], kv_a_norm_g[l],
            cat(gmq, _swap_halves(gmq[MLA_NOPE:])), gmk[:MLA_NOPE],
            cat(gmk_pe, gmk_pe_p, gmk_pe, gmk_pe_p),
            cat(ggq, _swap_halves(ggq)), cat(ggk, ggk), cat(ggk_p, ggk_p)))
        proj_consts = (proj_gains, win, wqb, wkvb)
        q, k, v = _projections(x, proj_consts, tables)

        o = _attention(q, k, v)

        n_a = MLA_HEADS * MLA_V
        wo = jnp.concatenate([w_o[l][:n_a], _gqa_slot_order(w_o[l][n_a:], 0)], axis=0).astype(bf)
        ffn_gains = _gain_rows((
            mla_out_norm_g[l], _gqa_slot_order(gqa_out_norm_g[l], 0), norm2_g[l]))
        ffn_consts = (ffn_gains, wo, w_gate[l].astype(bf), w_up[l].astype(bf), w_down[l].astype(bf))
        x = _merge_ffn(x, o, ffn_consts)
    return x
```

```python
import math

import jax
import jax.numpy as jnp
import numpy as np
from jax import lax
from jax.experimental import pallas as pl
from jax.experimental.pallas import tpu as pltpu

D_MODEL = 1024
GRID_W = 64
ROPE_THETA = 10000.0
EPS = 1e-6

MLA_HEADS = 8
Q_LORA = 384
KV_LORA = 256
MLA_NOPE = 64
MLA_ROPE = 32
MLA_V = 64
MLA_QK = MLA_NOPE + MLA_ROPE

GQA_HEADS = 8
GQA_KV_HEADS = 2
GQA_HD = 64
GQA_GROUP = GQA_HEADS // GQA_KV_HEADS

D_FF = 2816

LANES = 128
V_HALF = 64
LOG2E = math.log2(math.e)
VMEM_LIMIT_BYTES = 60 * 1024 * 1024

TB_PROJ = 1024
TS_PROJ = 256
TQ_BLOCK = 1024
TQ_ATTN = 512
TM_FFN = 1024

MLA_PAIRS = MLA_HEADS // 2
N_PAIRS = MLA_PAIRS + GQA_GROUP
Q_SLOTS = MLA_HEADS + GQA_HEADS
K_SLOTS = MLA_HEADS + GQA_KV_HEADS
V_SLOTS = MLA_PAIRS + 1

_P_CQ = (0, Q_LORA)
_P_CKV = (_P_CQ[1], _P_CQ[1] + KV_LORA)
_P_KPE = (_P_CKV[1], _P_CKV[1] + LANES)
_P_GQ = (_P_KPE[1], _P_KPE[1] + GQA_HEADS * LANES)
_P_GK = (_P_GQ[1], _P_GQ[1] + LANES)
_P_GK_P = (_P_GK[1], _P_GK[1] + LANES)
_P_GV = (_P_GK_P[1], _P_GK_P[1] + LANES)
P_WIDTH = _P_GV[1]

(_G_NORM1, _G_CQ, _G_CKV, _G_MLA_Q, _G_MLA_KN, _G_MLA_KP, _G_GQA_Q, _G_GQA_K, _G_GQA_KP) = range(9)
_G_OUT_A, _G_OUT_B, _G_NORM2 = range(3)
GAIN_ROWS = 16


def _bf16_dot(a, b):
    return jnp.dot(a.astype(jnp.bfloat16), b, preferred_element_type=jnp.float32)


def _rms_scale(x, width):
    return lax.rsqrt(jnp.sum(x * x, axis=-1, keepdims=True) * (1.0 / width) + EPS)


def _proj_kernel(x_ref, xn_ref, gain_ref, win_ref, wqb_ref, wkvb_ref,
                 tmq_ref, tmk_ref, tgq_ref, tgc_ref, tgs_ref,
                 q_ref, k_ref, v_ref, hb0, hb1, ql0, ql1, kv0, kv1, pr0, pr1):
    ts = ql0.shape[0]
    n_items = x_ref.shape[0] // ts
    assert n_items % 2 == 0
    hb, ql, kvs, pr = (hb0, hb1), (ql0, ql1), (kv0, kv1), (pr0, pr1)
    gain = lambda row, width: gain_ref[row:row + 1, :width]
    g1, gcq, gckv = gain(_G_NORM1, D_MODEL), gain(_G_CQ, Q_LORA), gain(_G_CKV, KV_LORA)
    gmq, gmkn, gmkp = gain(_G_MLA_Q, LANES), gain(_G_MLA_KN, LANES), gain(_G_MLA_KP, LANES)
    ggq, ggk, ggkp = gain(_G_GQA_Q, LANES), gain(_G_GQA_K, LANES), gain(_G_GQA_KP, LANES)

    def rows(it):
        return pl.ds(it * ts, ts)

    def stage_norm(src_ref, it, h_dst):
        x = src_ref[rows(it), :]
        h_dst[...] = (x * _rms_scale(x, D_MODEL) * g1).astype(h_dst.dtype)

    def stage_matmuls(h_src, ql_dst, kv_dst, pr_dst):
        p = jnp.dot(h_src[...], win_ref[...], preferred_element_type=jnp.float32)
        cq = p[:, _P_CQ[0]:_P_CQ[1]]
        ckv = p[:, _P_CKV[0]:_P_CKV[1]]
        ql_dst[...] = _bf16_dot(cq * _rms_scale(cq, Q_LORA) * gcq, wqb_ref[...])
        kv_dst[...] = _bf16_dot(ckv * _rms_scale(ckv, KV_LORA) * gckv, wkvb_ref[...])
        pr_dst[...] = p[:, _P_KPE[0]:]

    def stage_heads(it, ql_src, kv_src, pr_src):
        r = rows(it)
        off = lambda span: slice(span[0] - _P_KPE[0], span[1] - _P_KPE[0])
        lane = lax.broadcasted_iota(jnp.int32, (ts, LANES), 1)
        low = lane < LANES // 2

        y = pr_src[:, off(_P_KPE)] * gmkp * tmk_ref[r, :]
        kpe_rot = y + pltpu.roll(y, MLA_ROPE, 1)
        kpe = jnp.where(lane < MLA_ROPE, pr_src[:, off(_P_KPE)], 0.0)
        ss_pe = jnp.sum(kpe * kpe, axis=-1, keepdims=True)
        tmq = tmq_ref[r, :] * (gmq * (MLA_QK ** -0.5 * LOG2E))
        for hh in range(MLA_HEADS):
            sl = slice(hh * LANES, (hh + 1) * LANES)
            qh = ql_src[:, sl]
            qsq = jnp.where(lane < MLA_QK, qh * qh, 0.0)
            rq = lax.rsqrt(jnp.sum(qsq, axis=-1, keepdims=True) * (1.0 / MLA_QK) + EPS)
            q_ref[hh, r, :] = (qh * rq * tmq).astype(q_ref.dtype)
            kn = kv_src[:, sl]
            rk = lax.rsqrt((jnp.sum(kn * kn, axis=-1, keepdims=True) + ss_pe) * (1.0 / MLA_QK) + EPS)
            k_ref[hh, r, :] = (jnp.where(low, kn * gmkn, kpe_rot) * rk).astype(k_ref.dtype)
        for j in range(MLA_PAIRS):
            lo = MLA_HEADS * LANES + j * LANES
            v_ref[j, r, :] = kv_src[:, lo:lo + LANES].astype(v_ref.dtype)

        tgq = tgq_ref[r, :] * (ggq * (GQA_HD ** -0.5 * LOG2E))
        gq0 = off(_P_GQ).start
        for hh in range(GQA_HEADS):
            qh = pr_src[:, gq0 + hh * LANES:gq0 + (hh + 1) * LANES]
            q_ref[MLA_HEADS + hh, r, :] = (qh * _rms_scale(qh, LANES) * tgq).astype(q_ref.dtype)
        gk, gk_p = pr_src[:, off(_P_GK)], pr_src[:, off(_P_GK_P)]
        sq = gk * gk
        ss_lo = jnp.sum(jnp.where(low, sq, 0.0), axis=-1, keepdims=True)
        ss_hi = jnp.sum(jnp.where(low, 0.0, sq), axis=-1, keepdims=True)
        rk = jnp.where(low, lax.rsqrt(ss_lo * (1.0 / GQA_HD) + EPS), lax.rsqrt(ss_hi * (1.0 / GQA_HD) + EPS))
        rot = (gk * (ggk * tgc_ref[r, :]) + gk_p * (ggkp * tgs_ref[r, :])) * rk
        rot_x = pltpu.roll(rot, GQA_HD, 1)
        k_ref[MLA_HEADS, r, :] = jnp.where(low, rot, rot_x).astype(k_ref.dtype)
        k_ref[MLA_HEADS + 1, r, :] = jnp.where(low, rot_x, rot).astype(k_ref.dtype)
        v_ref[MLA_PAIRS, r, :] = pr_src[:, off(_P_GV)].astype(v_ref.dtype)

    @pl.when((pl.program_id(0) == 0) & (pl.program_id(1) == 0))
    def _():
        stage_norm(x_ref, 0, hb0)
        stage_norm(x_ref, 1, hb1)
        stage_matmuls(hb0, ql0, kv0, pr0)

    for it in range(n_items):
        cur, ahead = it % 2, it + 2
        src, idx = (x_ref, ahead) if ahead < n_items else (xn_ref, ahead - n_items)
        stage_norm(src, idx, hb[cur])
        stage_matmuls(hb[1 - cur], ql[1 - cur], kvs[1 - cur], pr[1 - cur])
        stage_heads(it, ql[cur], kvs[cur], pr[cur])


def _const_spec(shape):
    return pl.BlockSpec(shape, lambda *_: (0,) * len(shape), pipeline_mode=pl.Buffered(1))


def _projections(x, consts, tables):
    B, S, _ = x.shape
    tb, ts = TB_PROJ, TS_PROJ
    slots = lambda n: pl.BlockSpec((None, n, tb, LANES), lambda b, i: (b, 0, i, 0))
    tab = pl.BlockSpec((tb, LANES), lambda b, i: (i, 0))
    n_slots = (Q_SLOTS, K_SLOTS, V_SLOTS)
    widths = (MLA_HEADS * LANES, MLA_HEADS * (LANES + MLA_V), P_WIDTH - _P_KPE[0])
    n_i = S // tb

    def next_block(b, i):
        t = jnp.minimum(b * n_i + i + 1, B * n_i - 1)
        return t // n_i, t % n_i, 0

    return pl.pallas_call(
        _proj_kernel,
        grid=(B, n_i),
        in_specs=[pl.BlockSpec((None, tb, D_MODEL), lambda b, i: (b, i, 0)),
                  pl.BlockSpec((None, tb, D_MODEL), next_block)]
                 + [_const_spec(c.shape) for c in consts] + [tab] * len(tables),
        out_specs=[slots(n) for n in n_slots],
        out_shape=[jax.ShapeDtypeStruct((B, n, S, LANES), jnp.bfloat16) for n in n_slots],
        scratch_shapes=[pltpu.VMEM((ts, D_MODEL), jnp.bfloat16)] * 2
                       + [pltpu.VMEM((ts, w), jnp.float32) for w in widths for _ in range(2)],
        compiler_params=pltpu.CompilerParams(
            dimension_semantics=("arbitrary", "arbitrary"), vmem_limit_bytes=VMEM_LIMIT_BYTES),
        name="projections",
    )(x, x, *consts, *tables)


def _attn_kernel(q_ref, k_ref, v_ref, o_ref, sc0, sc1, mx0, mx1):
    tq, kv_len = sc0.shape[1:]
    n_items = (q_ref.shape[1] // tq) * N_PAIRS
    sc, mx = (sc0, sc1), (mx0, mx1)
    v_low = lax.broadcasted_iota(jnp.int32, (kv_len, LANES), 1) < V_HALF
    o_low = lax.broadcasted_iota(jnp.int32, (tq, LANES), 1) < V_HALF
    one = jnp.ones((), v_ref.dtype)

    def item(it):
        it = jnp.minimum(it, n_items - 1)
        return pl.multiple_of((it // N_PAIRS) * tq, tq), it % N_PAIRS

    def stage_scores(it, s_dst, m_dst):
        row0, j = item(it)
        mla = j < MLA_PAIRS
        for e in range(2):
            qi = jnp.where(mla, 2 * j + e, MLA_HEADS + j - MLA_PAIRS + e * GQA_GROUP)
            ki = jnp.where(mla, 2 * j + e, MLA_HEADS + e)
            s = lax.dot_general(q_ref[qi, pl.ds(row0, tq), :], k_ref[ki],
                                (((1,), (1,)), ((), ())), preferred_element_type=jnp.float32)
            s_dst[e] = s
            m_dst[e] = jnp.broadcast_to(jnp.max(s, axis=-1, keepdims=True), (tq, LANES))

    def stage_values(it, s_src, m_src):
        row0, j = item(it)
        vi = jnp.minimum(j, MLA_PAIRS)
        v = v_ref[vi]
        accs = []
        for e in range(2):
            m = m_src[e]
            p = jnp.concatenate([jnp.exp2(s_src[e, :, c:c + LANES] - m).astype(v.dtype)
                                 for c in range(0, kv_len, LANES)], axis=-1)
            vh = jnp.where(v_low, v, one) if e == 0 else jnp.where(v_low, one, v)
            accs.append(jnp.dot(p, vh, preferred_element_type=jnp.float32))
        num = jnp.where(o_low, accs[0], accs[1])
        den = jnp.where(o_low, pltpu.roll(accs[0], V_HALF, 1), pltpu.roll(accs[1], V_HALF, 1))
        o_ref[j, pl.ds(row0, tq), :] = (num / den).astype(o_ref.dtype)

    stage_scores(0, sc0, mx0)
    stage_scores(1, sc1, mx1)

    def trip(t, carry):
        stage_values(2 * t, sc0, mx0)
        stage_values(2 * t + 1, sc1, mx1)
        stage_scores(2 * t + 2, sc0, mx0)
        stage_scores(2 * t + 3, sc1, mx1)
        return carry

    lax.fori_loop(0, n_items // 2 - 1, trip, 0)
    stage_values(n_items - 2, sc0, mx0)
    stage_values(n_items - 1, sc1, mx1)


def _attention(q, k, v):
    B, _, S, _ = q.shape
    tb, tq = TQ_BLOCK, TQ_ATTN
    pair_buf = lambda w: pltpu.VMEM((2, tq, w), jnp.float32)
    return pl.pallas_call(
        _attn_kernel,
        grid=(B, S // tb),
        in_specs=[pl.BlockSpec((None, Q_SLOTS, tb, LANES), lambda b, i: (b, 0, i, 0)),
                  pl.BlockSpec((None, K_SLOTS, S, LANES), lambda b, i: (b, 0, 0, 0)),
                  pl.BlockSpec((None, V_SLOTS, S, LANES), lambda b, i: (b, 0, 0, 0))],
        out_specs=pl.BlockSpec((None, N_PAIRS, tb, LANES), lambda b, i: (b, 0, i, 0)),
        out_shape=jax.ShapeDtypeStruct((B, N_PAIRS, S, LANES), jnp.bfloat16),
        scratch_shapes=[pair_buf(S), pair_buf(S), pair_buf(LANES), pair_buf(LANES)],
        compiler_params=pltpu.CompilerParams(
            dimension_semantics=("parallel", "parallel"), vmem_limit_bytes=VMEM_LIMIT_BYTES),
        name="attention",
    )(q, k, v)


def _ffn_kernel(x_ref, o_ref, gain_ref, wo_ref, wg_ref, wu_ref, wd_ref, out_ref):
    oa = jnp.concatenate([o_ref[j] for j in range(MLA_PAIRS)], axis=-1).astype(jnp.float32)
    ob = jnp.concatenate([o_ref[j] for j in range(MLA_PAIRS, N_PAIRS)], axis=-1).astype(jnp.float32)
    na = oa * _rms_scale(oa, oa.shape[-1]) * gain_ref[_G_OUT_A:_G_OUT_A + 1, :oa.shape[-1]]
    nb = ob * _rms_scale(ob, ob.shape[-1]) * gain_ref[_G_OUT_B:_G_OUT_B + 1, :ob.shape[-1]]
    mixed = jnp.concatenate([na, nb], axis=-1)
    x1 = x_ref[...] + _bf16_dot(mixed, wo_ref[...])
    h2 = (x1 * _rms_scale(x1, D_MODEL) * gain_ref[_G_NORM2:_G_NORM2 + 1, :]).astype(jnp.bfloat16)
    g = jnp.dot(h2, wg_ref[...], preferred_element_type=jnp.float32)
    u = jnp.dot(h2, wu_ref[...], preferred_element_type=jnp.float32)
    silu = g / (1.0 + jnp.exp(-g))
    out_ref[...] = x1 + _bf16_dot(silu * u, wd_ref[...])


def _merge_ffn(x, o, consts):
    B, S, _ = x.shape
    tm = TM_FFN
    tok = pl.BlockSpec((None, tm, D_MODEL), lambda b, i: (b, i, 0))
    return pl.pallas_call(
        _ffn_kernel,
        grid=(B, S // tm),
        in_specs=[tok, pl.BlockSpec((None, N_PAIRS, tm, LANES), lambda b, i: (b, 0, i, 0))]
                 + [_const_spec(c.shape) for c in consts],
        out_specs=tok,
        out_shape=jax.ShapeDtypeStruct(x.shape, x.dtype),
        compiler_params=pltpu.CompilerParams(
            dimension_semantics=("parallel", "parallel"), vmem_limit_bytes=VMEM_LIMIT_BYTES),
        name="merge_ffn",
    )(x, o, *consts)


def _rope_tables(seq, d_rot):
    d_ax = d_rot // 2
    inv = np.float32(ROPE_THETA) ** (-(np.arange(0, d_ax, 2, dtype=np.float32) / np.float32(d_ax)))
    rows = seq // GRID_W
    row = np.repeat(np.arange(rows, dtype=np.float32), GRID_W)
    col = np.tile(np.arange(GRID_W, dtype=np.float32), rows)
    ang_r, ang_c = row[:, None] * inv[None, :], col[:, None] * inv[None, :]
    c = np.concatenate([np.cos(ang_r)] * 2 + [np.cos(ang_c)] * 2, axis=-1)
    s = np.concatenate([-np.sin(ang_r), np.sin(ang_r), -np.sin(ang_c), np.sin(ang_c)], axis=-1)
    return c.astype(np.float32), s.astype(np.float32)


def _swap_halves(a):
    n = a.shape[-1] // 4
    return jnp.concatenate([a[..., n:2 * n], a[..., :n], a[..., 3 * n:], a[..., 2 * n:3 * n]], axis=-1)


def _gain_rows(rows):
    rows = [jnp.pad(r.astype(jnp.float32), (0, D_MODEL - r.shape[0])) for r in rows]
    return jnp.pad(jnp.stack(rows), ((0, GAIN_ROWS - len(rows)), (0, 0)))


def _pad_lanes(v, front, width):
    return jnp.pad(v, [(0, 0)] * (v.ndim - 1) + [(front, width - front - v.shape[-1])])


def _gqa_slot_order(a, axis):
    shape = a.shape
    a = a.reshape(shape[:axis] + (GQA_KV_HEADS, GQA_GROUP, GQA_HD) + shape[axis + 1:])
    return a.swapaxes(axis, axis + 1).reshape(shape)


def kernel(x, norm1_g, w_in, q_a_norm_g, w_q_b, kv_a_norm_g, w_kv_b, mla_q_norm_g, mla_k_norm_g,
           gqa_q_norm_g, gqa_k_norm_g, mla_out_norm_g, gqa_out_norm_g, w_o, norm2_g, w_gate, w_up,
           w_down):
    B, S, _ = x.shape
    bf = jnp.bfloat16
    depth = w_in.shape[0]
    assert S % GRID_W == 0 and S % TB_PROJ == 0 and TB_PROJ % TS_PROJ == 0 and S % TM_FFN == 0
    assert S % TQ_BLOCK == 0 and TQ_BLOCK % TQ_ATTN == 0

    cm, sm = _rope_tables(S, MLA_ROPE)
    cg, sg = _rope_tables(S, GQA_HD)
    ncat = lambda *parts: jnp.asarray(np.concatenate(parts, axis=-1))
    tables = (ncat(np.ones((S, MLA_NOPE), np.float32), cm, sm),
              ncat(cm, sm, cm, sm),
              ncat(cg, sg), ncat(cg, cg), ncat(sg, sg))
    cat = lambda *parts: jnp.concatenate(parts, axis=-1)

    for l in range(depth):
        wi = w_in[l]
        o0 = Q_LORA + KV_LORA
        o1 = o0 + MLA_ROPE
        o2 = o1 + GQA_HEADS * GQA_HD
        o3 = o2 + GQA_KV_HEADS * GQA_HD
        w_kpe, w_gq = wi[:, o0:o1], wi[:, o1:o2].reshape(D_MODEL, GQA_HEADS, GQA_HD)
        w_gk = wi[:, o2:o3].reshape(D_MODEL, GQA_KV_HEADS, GQA_HD)
        win = cat(
            wi[:, :o0],
            w_kpe, _swap_halves(w_kpe), w_kpe, _swap_halves(w_kpe),
            cat(w_gq, _swap_halves(w_gq)).reshape(D_MODEL, GQA_HEADS * LANES),
            w_gk.reshape(D_MODEL, LANES), _swap_halves(w_gk).reshape(D_MODEL, LANES),
            wi[:, o3:]).astype(bf)
        wq = w_q_b[l].reshape(Q_LORA, MLA_HEADS, MLA_QK)
        wqb = cat(wq, _swap_halves(wq[..., MLA_NOPE:])).reshape(Q_LORA, MLA_HEADS * LANES).astype(bf)
        wkv = w_kv_b[l].reshape(KV_LORA, MLA_HEADS, MLA_NOPE + MLA_V)
        wkvb = cat(
            _pad_lanes(wkv[..., :MLA_NOPE], 0, LANES).reshape(KV_LORA, MLA_HEADS * LANES),
            wkv[..., MLA_NOPE:].reshape(KV_LORA, MLA_HEADS * MLA_V)).astype(bf)
        gmq, gmk, ggq, ggk = mla_q_norm_g[l], mla_k_norm_g[l], gqa_q_norm_g[l], gqa_k_norm_g[l]
        gmk_pe, gmk_pe_p, ggk_p = gmk[MLA_NOPE:], _swap_halves(gmk[MLA_NOPE:]), _swap_halves(ggk)
        proj_gains = _gain_rows((
            norm1_g[l], q_a_norm_g[l], kv_a_norm_g[l],
            cat(gmq, _swap_halves(gmq[MLA_NOPE:])), gmk[:MLA_NOPE],
            cat(gmk_pe, gmk_pe_p, gmk_pe, gmk_pe_p),
            cat(ggq, _swap_halves(ggq)), cat(ggk, ggk), cat(ggk_p, ggk_p)))
        proj_consts = (proj_gains, win, wqb, wkvb)
        q, k, v = _projections(x, proj_consts, tables)

        o = _attention(q, k, v)

        n_a = MLA_HEADS * MLA_V
        wo = jnp.concatenate([w_o[l][:n_a], _gqa_slot_order(w_o[l][n_a:], 0)], axis=0).astype(bf)
        ffn_gains = _gain_rows((
            mla_out_norm_g[l], _gqa_slot_order(gqa_out_norm_g[l], 0), norm2_g[l]))
        ffn_consts = (ffn_gains, wo, w_gate[l].astype(bf), w_up[l].astype(bf), w_down[l].astype(bf))
        x = _merge_ffn(x, o, ffn_consts)
    return x
```

```python
import math

import jax
import jax.numpy as jnp
import numpy as np
from jax import lax
from jax.experimental import pallas as pl
from jax.experimental.pallas import tpu as pltpu

D_MODEL = 1024
GRID_W = 64
ROPE_THETA = 10000.0
EPS = 1e-6

MLA_HEADS = 8
Q_LORA = 384
KV_LORA = 256
MLA_NOPE = 64
MLA_ROPE = 32
MLA_V = 64
MLA_QK = MLA_NOPE + MLA_ROPE

GQA_HEADS = 8
GQA_KV_HEADS = 2
GQA_HD = 64
GQA_GROUP = GQA_HEADS // GQA_KV_HEADS

D_FF = 2816

LANES = 128
V_HALF = 64
LOG2E = math.log2(math.e)
VMEM_LIMIT_BYTES = 60 * 1024 * 1024

TB_PROJ = 1024
TS_PROJ = 512
TQ_BLOCK = 2048
TQ_ATTN = 512
TM_FFN = 1024

MLA_PAIRS = MLA_HEADS // 2
N_PAIRS = MLA_PAIRS + GQA_GROUP
Q_SLOTS = MLA_HEADS + GQA_HEADS
K_SLOTS = MLA_HEADS + GQA_KV_HEADS
V_SLOTS = MLA_PAIRS + 1

_P_CQ = (0, Q_LORA)
_P_CKV = (_P_CQ[1], _P_CQ[1] + KV_LORA)
_P_KPE = (_P_CKV[1], _P_CKV[1] + LANES)
_P_GQ = (_P_KPE[1], _P_KPE[1] + GQA_HEADS * LANES)
_P_GK = (_P_GQ[1], _P_GQ[1] + LANES)
_P_GK_P = (_P_GK[1], _P_GK[1] + LANES)
_P_GV = (_P_GK_P[1], _P_GK_P[1] + LANES)
P_WIDTH = _P_GV[1]

(_G_NORM1, _G_CQ, _G_CKV, _G_MLA_Q, _G_MLA_KN, _G_MLA_KP, _G_GQA_Q, _G_GQA_K, _G_GQA_KP) = range(9)
_G_OUT_A, _G_OUT_B, _G_NORM2 = range(3)
GAIN_ROWS = 16


def _bf16_dot(a, b):
    return jnp.dot(a.astype(jnp.bfloat16), b, preferred_element_type=jnp.float32)


def _rms_scale(x, width):
    return lax.rsqrt(jnp.sum(x * x, axis=-1, keepdims=True) * (1.0 / width) + EPS)


def _proj_kernel(x_ref, xn_ref, gain_ref, win_ref, wqb_ref, wkvb_ref,
                 tmq_ref, tmk_ref, tgq_ref, tgc_ref, tgs_ref,
                 q_ref, k_ref, v_ref, hb0, hb1, ql0, ql1, kv0, kv1, pr0, pr1):
    ts = ql0.shape[0]
    n_items = xn_ref.shape[0] // ts
    assert n_items == 2
    hb, ql, kvs, pr = (hb0, hb1), (ql0, ql1), (kv0, kv1), (pr0, pr1)
    gain = lambda row, width: gain_ref[row:row + 1, :width]
    g1, gcq, gckv = gain(_G_NORM1, D_MODEL), gain(_G_CQ, Q_LORA), gain(_G_CKV, KV_LORA)
    gmq, gmkn, gmkp = gain(_G_MLA_Q, LANES), gain(_G_MLA_KN, LANES), gain(_G_MLA_KP, LANES)
    ggq, ggk, ggkp = gain(_G_GQA_Q, LANES), gain(_G_GQA_K, LANES), gain(_G_GQA_KP, LANES)

    def rows(it):
        return pl.ds(it * ts, ts)

    def stage_norm(src_ref, it, h_dst):
        x = src_ref[rows(it), :]
        h_dst[...] = (x * _rms_scale(x, D_MODEL) * g1).astype(h_dst.dtype)

    def stage_matmuls(h_src, ql_dst, kv_dst, pr_dst):
        p = jnp.dot(h_src[...], win_ref[...], preferred_element_type=jnp.float32)
        cq = p[:, _P_CQ[0]:_P_CQ[1]]
        ckv = p[:, _P_CKV[0]:_P_CKV[1]]
        ql_dst[...] = _bf16_dot(cq * _rms_scale(cq, Q_LORA) * gcq, wqb_ref[...])
        kv_dst[...] = _bf16_dot(ckv * _rms_scale(ckv, KV_LORA) * gckv, wkvb_ref[...])
        pr_dst[...] = p[:, _P_KPE[0]:]

    def stage_heads(it, ql_src, kv_src, pr_src):
        r = rows(it)
        off = lambda span: slice(span[0] - _P_KPE[0], span[1] - _P_KPE[0])
        lane = lax.broadcasted_iota(jnp.int32, (ts, LANES), 1)
        low = lane < LANES // 2

        y = pr_src[:, off(_P_KPE)] * gmkp * tmk_ref[r, :]
        kpe_rot = y + pltpu.roll(y, MLA_ROPE, 1)
        kpe = jnp.where(lane < MLA_ROPE, pr_src[:, off(_P_KPE)], 0.0)
        ss_pe = jnp.sum(kpe * kpe, axis=-1, keepdims=True)
        tmq = tmq_ref[r, :] * (gmq * (MLA_QK ** -0.5 * LOG2E))
        for hh in range(MLA_HEADS):
            sl = slice(hh * LANES, (hh + 1) * LANES)
            qh = ql_src[:, sl]
            qsq = jnp.where(lane < MLA_QK, qh * qh, 0.0)
            rq = lax.rsqrt(jnp.sum(qsq, axis=-1, keepdims=True) * (1.0 / MLA_QK) + EPS)
            q_ref[hh, r, :] = (qh * rq * tmq).astype(q_ref.dtype)
            kn = kv_src[:, sl]
            rk = lax.rsqrt((jnp.sum(kn * kn, axis=-1, keepdims=True) + ss_pe) * (1.0 / MLA_QK) + EPS)
            k_ref[hh, r, :] = (jnp.where(low, kn * gmkn, kpe_rot) * rk).astype(k_ref.dtype)
        for j in range(MLA_PAIRS):
            lo = MLA_HEADS * LANES + j * LANES
            v_ref[j, r, :] = kv_src[:, lo:lo + LANES].astype(v_ref.dtype)

        tgq = tgq_ref[r, :] * (ggq * (GQA_HD ** -0.5 * LOG2E))
        gq0 = off(_P_GQ).start
        for hh in range(GQA_HEADS):
            qh = pr_src[:, gq0 + hh * LANES:gq0 + (hh + 1) * LANES]
            q_ref[MLA_HEADS + hh, r, :] = (qh * _rms_scale(qh, LANES) * tgq).astype(q_ref.dtype)
        gk, gk_p = pr_src[:, off(_P_GK)], pr_src[:, off(_P_GK_P)]
        sq = gk * gk
        ss_lo = jnp.sum(jnp.where(low, sq, 0.0), axis=-1, keepdims=True)
        ss_hi = jnp.sum(jnp.where(low, 0.0, sq), axis=-1, keepdims=True)
        rk = jnp.where(low, lax.rsqrt(ss_lo * (1.0 / GQA_HD) + EPS), lax.rsqrt(ss_hi * (1.0 / GQA_HD) + EPS))
        rot = (gk * (ggk * tgc_ref[r, :]) + gk_p * (ggkp * tgs_ref[r, :])) * rk
        rot_x = pltpu.roll(rot, GQA_HD, 1)
        k_ref[MLA_HEADS, r, :] = jnp.where(low, rot, rot_x).astype(k_ref.dtype)
        k_ref[MLA_HEADS + 1, r, :] = jnp.where(low, rot_x, rot).astype(k_ref.dtype)
        v_ref[MLA_PAIRS, r, :] = pr_src[:, off(_P_GV)].astype(v_ref.dtype)

    @pl.when((pl.program_id(0) == 0) & (pl.program_id(1) == 0))
    def _():
        stage_norm(x_ref, 0, hb0)
        stage_norm(x_ref, 1, hb1)
        stage_matmuls(hb0, ql0, kv0, pr0)

    for it in range(n_items):
        cur = it % 2
        stage_norm(xn_ref, it, hb[cur])
        stage_matmuls(hb[1 - cur], ql[1 - cur], kvs[1 - cur], pr[1 - cur])
        stage_heads(it, ql[cur], kvs[cur], pr[cur])


def _const_spec(shape):
    return pl.BlockSpec(shape, lambda *_: (0,) * len(shape), pipeline_mode=pl.Buffered(1))


def _projections(x, consts, tables):
    B, S, _ = x.shape
    tb, ts = TB_PROJ, TS_PROJ
    slots = lambda n: pl.BlockSpec((None, n, tb, LANES), lambda b, i: (b, 0, i, 0))
    tab = pl.BlockSpec((tb, LANES), lambda b, i: (i, 0))
    n_slots = (Q_SLOTS, K_SLOTS, V_SLOTS)
    widths = (MLA_HEADS * LANES, MLA_HEADS * (LANES + MLA_V), P_WIDTH - _P_KPE[0])
    n_i = S // tb

    def next_block(b, i):
        t = jnp.minimum(b * n_i + i + 1, B * n_i - 1)
        return t // n_i, t % n_i, 0

    return pl.pallas_call(
        _proj_kernel,
        grid=(B, n_i),
        in_specs=[pl.BlockSpec((None, tb, D_MODEL), lambda b, i: (0, 0, 0), pipeline_mode=pl.Buffered(1)),
                  pl.BlockSpec((None, tb, D_MODEL), next_block)]
                 + [_const_spec(c.shape) for c in consts] + [tab] * len(tables),
        out_specs=[slots(n) for n in n_slots],
        out_shape=[jax.ShapeDtypeStruct((B, n, S, LANES), jnp.bfloat16) for n in n_slots],
        scratch_shapes=[pltpu.VMEM((ts, D_MODEL), jnp.bfloat16)] * 2
                       + [pltpu.VMEM((ts, w), jnp.float32) for w in widths for _ in range(2)],
        compiler_params=pltpu.CompilerParams(
            dimension_semantics=("arbitrary", "arbitrary"), vmem_limit_bytes=VMEM_LIMIT_BYTES),
        name="projections",
    )(x, x, *consts, *tables)


def _attn_kernel(q_ref, k_ref, v_ref, o_ref, sc0, sc1, mx0, mx1):
    tq, kv_len = sc0.shape[1:]
    n_items = (q_ref.shape[1] // tq) * N_PAIRS
    sc, mx = (sc0, sc1), (mx0, mx1)
    v_low = lax.broadcasted_iota(jnp.int32, (kv_len, LANES), 1) < V_HALF
    o_low = lax.broadcasted_iota(jnp.int32, (tq, LANES), 1) < V_HALF
    one = jnp.ones((), v_ref.dtype)

    def item(it):
        it = jnp.minimum(it, n_items - 1)
        return pl.multiple_of((it // N_PAIRS) * tq, tq), it % N_PAIRS

    def stage_scores(it, s_dst, m_dst):
        row0, j = item(it)
        mla = j < MLA_PAIRS
        for e in range(2):
            qi = jnp.where(mla, 2 * j + e, MLA_HEADS + j - MLA_PAIRS + e * GQA_GROUP)
            ki = jnp.where(mla, 2 * j + e, MLA_HEADS + e)
            s = lax.dot_general(q_ref[qi, pl.ds(row0, tq), :], k_ref[ki],
                                (((1,), (1,)), ((), ())), preferred_element_type=jnp.float32)
            s_dst[e] = s
            m_dst[e] = jnp.broadcast_to(jnp.max(s, axis=-1, keepdims=True), (tq, LANES))

    def stage_values(it, s_src, m_src):
        row0, j = item(it)
        vi = jnp.minimum(j, MLA_PAIRS)
        v = v_ref[vi]
        accs = []
        for e in range(2):
            m = m_src[e]
            p = jnp.concatenate([jnp.exp2(s_src[e, :, c:c + LANES] - m).astype(v.dtype)
                                 for c in range(0, kv_len, LANES)], axis=-1)
            vh = jnp.where(v_low, v, one) if e == 0 else jnp.where(v_low, one, v)
            accs.append(jnp.dot(p, vh, preferred_element_type=jnp.float32))
        num = jnp.where(o_low, accs[0], accs[1])
        den = jnp.where(o_low, pltpu.roll(accs[0], V_HALF, 1), pltpu.roll(accs[1], V_HALF, 1))
        o_ref[j, pl.ds(row0, tq), :] = (num / den).astype(o_ref.dtype)

    stage_scores(0, sc0, mx0)
    stage_scores(1, sc1, mx1)

    def trip(t, carry):
        stage_values(2 * t, sc0, mx0)
        stage_values(2 * t + 1, sc1, mx1)
        stage_scores(2 * t + 2, sc0, mx0)
        stage_scores(2 * t + 3, sc1, mx1)
        return carry

    lax.fori_loop(0, n_items // 2 - 1, trip, 0)
    stage_values(n_items - 2, sc0, mx0)
    stage_values(n_items - 1, sc1, mx1)


def _attention(q, k, v):
    B, _, S, _ = q.shape
    tb, tq = TQ_BLOCK, TQ_ATTN
    pair_buf = lambda w: pltpu.VMEM((2, tq, w), jnp.float32)
    return pl.pallas_call(
        _attn_kernel,
        grid=(B, S // tb),
        in_specs=[pl.BlockSpec((None, Q_SLOTS, tb, LANES), lambda b, i: (b, 0, i, 0)),
                  pl.BlockSpec((None, K_SLOTS, S, LANES), lambda b, i: (b, 0, 0, 0)),
                  pl.BlockSpec((None, V_SLOTS, S, LANES), lambda b, i: (b, 0, 0, 0))],
        out_specs=pl.BlockSpec((None, N_PAIRS, tb, LANES), lambda b, i: (b, 0, i, 0)),
        out_shape=jax.ShapeDtypeStruct((B, N_PAIRS, S, LANES), jnp.bfloat16),
        scratch_shapes=[pair_buf(S), pair_buf(S), pair_buf(LANES), pair_buf(LANES)],
        compiler_params=pltpu.CompilerParams(
            dimension_semantics=("parallel", "parallel"), vmem_limit_bytes=VMEM_LIMIT_BYTES),
        name="attention",
    )(q, k, v)


def _ffn_kernel(x_ref, o_ref, gain_ref, wo_ref, wg_ref, wu_ref, wd_ref, out_ref):
    oa = jnp.concatenate([o_ref[j] for j in range(MLA_PAIRS)], axis=-1).astype(jnp.float32)
    ob = jnp.concatenate([o_ref[j] for j in range(MLA_PAIRS, N_PAIRS)], axis=-1).astype(jnp.float32)
    na = oa * _rms_scale(oa, oa.shape[-1]) * gain_ref[_G_OUT_A:_G_OUT_A + 1, :oa.shape[-1]]
    nb = ob * _rms_scale(ob, ob.shape[-1]) * gain_ref[_G_OUT_B:_G_OUT_B + 1, :ob.shape[-1]]
    mixed = jnp.concatenate([na, nb], axis=-1)
    x1 = x_ref[...] + _bf16_dot(mixed, wo_ref[...])
    h2 = (x1 * _rms_scale(x1, D_MODEL) * gain_ref[_G_NORM2:_G_NORM2 + 1, :]).astype(jnp.bfloat16)
    g = jnp.dot(h2, wg_ref[...], preferred_element_type=jnp.float32)
    u = jnp.dot(h2, wu_ref[...], preferred_element_type=jnp.float32)
    silu = g / (1.0 + jnp.exp(-g))
    out_ref[...] = x1 + _bf16_dot(silu * u, wd_ref[...])


def _merge_ffn(x, o, consts):
    B, S, _ = x.shape
    tm = TM_FFN
    tok = pl.BlockSpec((None, tm, D_MODEL), lambda b, i: (b, i, 0))
    return pl.pallas_call(
        _ffn_kernel,
        grid=(B, S // tm),
        in_specs=[tok, pl.BlockSpec((None, N_PAIRS, tm, LANES), lambda b, i: (b, 0, i, 0))]
                 + [_const_spec(c.shape) for c in consts],
        out_specs=tok,
        out_shape=jax.ShapeDtypeStruct(x.shape, x.dtype),
        compiler_params=pltpu.CompilerParams(
            dimension_semantics=("parallel", "parallel"), vmem_limit_bytes=VMEM_LIMIT_BYTES),
        name="merge_ffn",
    )(x, o, *consts)


def _rope_tables(seq, d_rot):
    d_ax = d_rot // 2
    inv = np.float32(ROPE_THETA) ** (-(np.arange(0, d_ax, 2, dtype=np.float32) / np.float32(d_ax)))
    rows = seq // GRID_W
    row = np.repeat(np.arange(rows, dtype=np.float32), GRID_W)
    col = np.tile(np.arange(GRID_W, dtype=np.float32), rows)
    ang_r, ang_c = row[:, None] * inv[None, :], col[:, None] * inv[None, :]
    c = np.concatenate([np.cos(ang_r)] * 2 + [np.cos(ang_c)] * 2, axis=-1)
    s = np.concatenate([-np.sin(ang_r), np.sin(ang_r), -np.sin(ang_c), np.sin(ang_c)], axis=-1)
    return c.astype(np.float32), s.astype(np.float32)


def _swap_halves(a):
    n = a.shape[-1] // 4
    return jnp.concatenate([a[..., n:2 * n], a[..., :n], a[..., 3 * n:], a[..., 2 * n:3 * n]], axis=-1)


def _gain_rows(rows):
    rows = [jnp.pad(r.astype(jnp.float32), (0, D_MODEL - r.shape[0])) for r in rows]
    return jnp.pad(jnp.stack(rows), ((0, GAIN_ROWS - len(rows)), (0, 0)))


def _pad_lanes(v, front, width):
    return jnp.pad(v, [(0, 0)] * (v.ndim - 1) + [(front, width - front - v.shape[-1])])


def _gqa_slot_order(a, axis):
    shape = a.shape
    a = a.reshape(shape[:axis] + (GQA_KV_HEADS, GQA_GROUP, GQA_HD) + shape[axis + 1:])
    return a.swapaxes(axis, axis + 1).reshape(shape)


def kernel(x, norm1_g, w_in, q_a_norm_g, w_q_b, kv_a_norm_g, w_kv_b, mla_q_norm_g, mla_k_norm_g,
           gqa_q_norm_g, gqa_k_norm_g, mla_out_norm_g, gqa_out_norm_g, w_o, norm2_g, w_gate, w_up,
           w_down):
    B, S, _ = x.shape
    bf = jnp.bfloat16
    depth = w_in.shape[0]
    assert S % GRID_W == 0 and S % TB_PROJ == 0 and TB_PROJ % TS_PROJ == 0 and S % TM_FFN == 0
    assert S % TQ_BLOCK == 0 and TQ_BLOCK % TQ_ATTN == 0

    cm, sm = _rope_tables(S, MLA_ROPE)
    cg, sg = _rope_tables(S, GQA_HD)
    ncat = lambda *parts: jnp.asarray(np.concatenate(parts, axis=-1))
    tables = (ncat(np.ones((S, MLA_NOPE), np.float32), cm, sm),
              ncat(cm, sm, cm, sm),
              ncat(cg, sg), ncat(cg, cg), ncat(sg, sg))
    cat = lambda *parts: jnp.concatenate(parts, axis=-1)

    for l in range(depth):
        wi = w_in[l]
        o0 = Q_LORA + KV_LORA
        o1 = o0 + MLA_ROPE
        o2 = o1 + GQA_HEADS * GQA_HD
        o3 = o2 + GQA_KV_HEADS * GQA_HD
        w_kpe, w_gq = wi[:, o0:o1], wi[:, o1:o2].reshape(D_MODEL, GQA_HEADS, GQA_HD)
        w_gk = wi[:, o2:o3].reshape(D_MODEL, GQA_KV_HEADS, GQA_HD)
        win = cat(
            wi[:, :o0],
            w_kpe, _swap_halves(w_kpe), w_kpe, _swap_halves(w_kpe),
            cat(w_gq, _swap_halves(w_gq)).reshape(D_MODEL, GQA_HEADS * LANES),
            w_gk.reshape(D_MODEL, LANES), _swap_halves(w_gk).reshape(D_MODEL, LANES),
            wi[:, o3:]).astype(bf)
        wq = w_q_b[l].reshape(Q_LORA, MLA_HEADS, MLA_QK)
        wqb = cat(wq, _swap_halves(wq[..., MLA_NOPE:])).reshape(Q_LORA, MLA_HEADS * LANES).astype(bf)
        wkv = w_kv_b[l].reshape(KV_LORA, MLA_HEADS, MLA_NOPE + MLA_V)
        wkvb = cat(
            _pad_lanes(wkv[..., :MLA_NOPE], 0, LANES).reshape(KV_LORA, MLA_HEADS * LANES),
            wkv[..., MLA_NOPE:].reshape(KV_LORA, MLA_HEADS * MLA_V)).astype(bf)
        gmq, gmk, ggq, ggk = mla_q_norm_g[l], mla_k_norm_g[l], gqa_q_norm_g[l], gqa_k_norm_g[l]
        gmk_pe, gmk_pe_p, ggk_p = gmk[MLA_NOPE:], _swap_halves(gmk[MLA_NOPE:]), _swap_halves(ggk)
        proj_gains = _gain_rows((
            norm1_g[l], q_a_norm_g[l], kv_a_norm_g[l],
            cat(gmq, _swap_halves(gmq[MLA_NOPE:])), gmk[:MLA_NOPE],
            cat(gmk_pe, gmk_pe_p, gmk_pe, gmk_pe_p),
            cat(ggq, _swap_halves(ggq)), cat(ggk, ggk), cat(ggk_p, ggk_p)))
        proj_consts = (proj_gains, win, wqb, wkvb)
        q, k, v = _projections(x, proj_consts, tables)

        o = _attention(q, k, v)

        n_a = MLA_HEADS * MLA_V
        wo = jnp.concatenate([w_o[l][:n_a], _gqa_slot_order(w_o[l][n_a:], 0)], axis=0).astype(bf)
        ffn_gains = _gain_rows((
            mla_out_norm_g[l], _gqa_slot_order(gqa_out_norm_g[l], 0), norm2_g[l]))
        ffn_consts = (ffn_gains, wo, w_gate[l].astype(bf), w_up[l].astype(bf), w_down[l].astype(bf))
        x = _merge_ffn(x, o, ffn_consts)
    return x
```

```python
import math

import jax
import jax.numpy as jnp
import numpy as np
from jax import lax
from jax.experimental import pallas as pl
from jax.experimental.pallas import tpu as pltpu

D_MODEL = 1024
GRID_W = 64
ROPE_THETA = 10000.0
EPS = 1e-6

MLA_HEADS = 8
Q_LORA = 384
KV_LORA = 256
MLA_NOPE = 64
MLA_ROPE = 32
MLA_V = 64
MLA_QK = MLA_NOPE + MLA_ROPE

GQA_HEADS = 8
GQA_KV_HEADS = 2
GQA_HD = 64
GQA_GROUP = GQA_HEADS // GQA_KV_HEADS

D_FF = 2816

LANES = 128
V_HALF = 64
LOG2E = math.log2(math.e)
VMEM_LIMIT_BYTES = 60 * 1024 * 1024

TB_PROJ = 1024
TS_PROJ = 512
TQ_BLOCK = 2048
TQ_ATTN = 512
TM_FFN = 1024

MLA_PAIRS = MLA_HEADS // 2
N_PAIRS = MLA_PAIRS + GQA_GROUP
Q_SLOTS = MLA_HEADS + GQA_HEADS
K_SLOTS = MLA_HEADS + GQA_KV_HEADS
V_SLOTS = MLA_PAIRS + GQA_KV_HEADS
assert MLA_QK + MLA_ROPE == LANES and MLA_NOPE + MLA_V == LANES and 2 * GQA_HD == LANES
assert MLA_HEADS % 2 == 0 and GQA_GROUP % 2 == 0

_P_CQ = (0, Q_LORA)
_P_CKV = (_P_CQ[1], _P_CQ[1] + KV_LORA)
_P_KPE = (_P_CKV[1], _P_CKV[1] + LANES)
_P_GQ = (_P_KPE[1], _P_KPE[1] + GQA_HEADS * LANES)
_P_GK = (_P_GQ[1], _P_GQ[1] + LANES)
_P_GK_P = (_P_GK[1], _P_GK[1] + LANES)
_P_GV = (_P_GK_P[1], _P_GK_P[1] + LANES)
P_WIDTH = _P_GV[1]

(_G_NORM1, _G_CQ, _G_CKV, _G_MLA_Q, _G_MLA_KN, _G_MLA_KP, _G_GQA_Q, _G_GQA_K, _G_GQA_KP) = range(9)
_G_OUT_A, _G_OUT_B, _G_NORM2 = range(3)
GAIN_ROWS = 16


def _bf16_dot(a, b):
    return jnp.dot(a.astype(jnp.bfloat16), b, preferred_element_type=jnp.float32)


def _rms_scale(x, width):
    return lax.rsqrt(jnp.sum(x * x, axis=-1, keepdims=True) * (1.0 / width) + EPS)


def _proj_kernel(x_ref, xn_ref, gain_ref, win_ref, wqb_ref, wkvb_ref,
                 tmq_ref, tmk_ref, tgq_ref, tgc_ref, tgs_ref,
                 q_ref, k_ref, v_ref, hb0, hb1, ql0, ql1, kv0, kv1, pr0, pr1):
    ts = ql0.shape[0]
    n_items = xn_ref.shape[0] // ts
    assert n_items == 2
    hb, ql, kvs, pr = (hb0, hb1), (ql0, ql1), (kv0, kv1), (pr0, pr1)
    gain = lambda row, width: gain_ref[row:row + 1, :width]
    g1, gcq, gckv = gain(_G_NORM1, D_MODEL), gain(_G_CQ, Q_LORA), gain(_G_CKV, KV_LORA)
    gmq, gmkn, gmkp = gain(_G_MLA_Q, LANES), gain(_G_MLA_KN, LANES), gain(_G_MLA_KP, LANES)
    ggq, ggk, ggkp = gain(_G_GQA_Q, LANES), gain(_G_GQA_K, LANES), gain(_G_GQA_KP, LANES)

    def rows(it):
        return pl.ds(it * ts, ts)

    def stage_norm(src_ref, it, h_dst):
        x = src_ref[rows(it), :]
        h_dst[...] = (x * _rms_scale(x, D_MODEL) * g1).astype(h_dst.dtype)

    def stage_matmuls(h_src, ql_dst, kv_dst, pr_dst):
        p = jnp.dot(h_src[...], win_ref[...], preferred_element_type=jnp.float32)
        cq = p[:, _P_CQ[0]:_P_CQ[1]]
        ckv = p[:, _P_CKV[0]:_P_CKV[1]]
        ql_dst[...] = _bf16_dot(cq * _rms_scale(cq, Q_LORA) * gcq, wqb_ref[...])
        kv_dst[...] = _bf16_dot(ckv * _rms_scale(ckv, KV_LORA) * gckv, wkvb_ref[...])
        pr_dst[...] = p[:, _P_KPE[0]:]

    def stage_heads(it, ql_src, kv_src, pr_src):
        r = rows(it)
        off = lambda span: slice(span[0] - _P_KPE[0], span[1] - _P_KPE[0])
        lane = lax.broadcasted_iota(jnp.int32, (ts, LANES), 1)
        low = lane < LANES // 2

        y = pr_src[:, off(_P_KPE)] * gmkp * tmk_ref[r, :]
        kpe_rot = y + pltpu.roll(y, MLA_ROPE, 1)
        kpe = jnp.where(lane < MLA_ROPE, pr_src[:, off(_P_KPE)], 0.0)
        ss_pe = jnp.sum(kpe * kpe, axis=-1, keepdims=True)
        tmq = tmq_ref[r, :] * (gmq * (MLA_QK ** -0.5 * LOG2E))
        for hh in range(MLA_HEADS):
            sl = slice(hh * LANES, (hh + 1) * LANES)
            qh = ql_src[:, sl]
            qsq = jnp.where(lane < MLA_QK, qh * qh, 0.0)
            rq = lax.rsqrt(jnp.sum(qsq, axis=-1, keepdims=True) * (1.0 / MLA_QK) + EPS)
            q_ref[hh, r, :] = (qh * rq * tmq).astype(q_ref.dtype)
            kn = kv_src[:, sl]
            ss_kn = jnp.sum(jnp.where(low, kn * kn, 0.0), axis=-1, keepdims=True)
            rk = lax.rsqrt((ss_kn + ss_pe) * (1.0 / MLA_QK) + EPS)
            k_ref[hh, r, :] = (jnp.where(low, kn * gmkn, kpe_rot) * rk).astype(k_ref.dtype)
        for j in range(MLA_PAIRS):
            even = kv_src[:, 2 * j * LANES:(2 * j + 1) * LANES]
            odd = kv_src[:, (2 * j + 1) * LANES:(2 * j + 2) * LANES]
            v_ref[j, r, :] = jnp.where(low, pltpu.roll(even, MLA_V, 1), odd).astype(v_ref.dtype)

        tgq = tgq_ref[r, :] * (ggq * (GQA_HD ** -0.5 * LOG2E))
        gq0 = off(_P_GQ).start
        for hh in range(GQA_HEADS):
            qh = pr_src[:, gq0 + hh * LANES:gq0 + (hh + 1) * LANES]
            q_ref[MLA_HEADS + hh, r, :] = (qh * _rms_scale(qh, LANES) * tgq).astype(q_ref.dtype)
        gk, gk_p = pr_src[:, off(_P_GK)], pr_src[:, off(_P_GK_P)]
        sq = gk * gk
        ss_lo = jnp.sum(jnp.where(low, sq, 0.0), axis=-1, keepdims=True)
        ss_hi = jnp.sum(jnp.where(low, 0.0, sq), axis=-1, keepdims=True)
        rk = jnp.where(low, lax.rsqrt(ss_lo * (1.0 / GQA_HD) + EPS), lax.rsqrt(ss_hi * (1.0 / GQA_HD) + EPS))
        rot = (gk * (ggk * tgc_ref[r, :]) + gk_p * (ggkp * tgs_ref[r, :])) * rk
        rot_x = pltpu.roll(rot, GQA_HD, 1)
        k_ref[MLA_HEADS, r, :] = jnp.where(low, rot, rot_x).astype(k_ref.dtype)
        k_ref[MLA_HEADS + 1, r, :] = jnp.where(low, rot_x, rot).astype(k_ref.dtype)
        gv = pr_src[:, off(_P_GV)]
        gv_x = pltpu.roll(gv, GQA_HD, 1)
        v_ref[MLA_PAIRS, r, :] = jnp.where(low, gv, gv_x).astype(v_ref.dtype)
        v_ref[MLA_PAIRS + 1, r, :] = jnp.where(low, gv_x, gv).astype(v_ref.dtype)

    @pl.when((pl.program_id(0) == 0) & (pl.program_id(1) == 0))
    def _():
        stage_norm(x_ref, 0, hb0)
        stage_norm(x_ref, 1, hb1)
        stage_matmuls(hb0, ql0, kv0, pr0)

    for it in range(n_items):
        cur = it % 2
        stage_norm(xn_ref, it, hb[cur])
        stage_matmuls(hb[1 - cur], ql[1 - cur], kvs[1 - cur], pr[1 - cur])
        stage_heads(it, ql[cur], kvs[cur], pr[cur])


def _const_spec(shape):
    return pl.BlockSpec(shape, lambda *_: (0,) * len(shape), pipeline_mode=pl.Buffered(1))


def _projections(x, consts, tables):
    B, S, _ = x.shape
    tb, ts = TB_PROJ, TS_PROJ
    slots = lambda n: pl.BlockSpec((None, n, tb, LANES), lambda b, i: (b, 0, i, 0))
    tab = pl.BlockSpec((tb, LANES), lambda b, i: (i, 0))
    n_slots = (Q_SLOTS, K_SLOTS, V_SLOTS)
    widths = (MLA_HEADS * LANES, MLA_HEADS * (MLA_NOPE + MLA_V), P_WIDTH - _P_KPE[0])
    n_i = S // tb

    def next_block(b, i):
        t = jnp.minimum(b * n_i + i + 1, B * n_i - 1)
        return t // n_i, t % n_i, 0

    return pl.pallas_call(
        _proj_kernel,
        grid=(B, n_i),
        in_specs=[pl.BlockSpec((None, tb, D_MODEL), lambda b, i: (0, 0, 0), pipeline_mode=pl.Buffered(1)),
                  pl.BlockSpec((None, tb, D_MODEL), next_block)]
                 + [_const_spec(c.shape) for c in consts] + [tab] * len(tables),
        out_specs=[slots(n) for n in n_slots],
        out_shape=[jax.ShapeDtypeStruct((B, n, S, LANES), jnp.bfloat16) for n in n_slots],
        scratch_shapes=[pltpu.VMEM((ts, D_MODEL), jnp.bfloat16)] * 2
                       + [pltpu.VMEM((ts, w), jnp.float32) for w in widths for _ in range(2)],
        compiler_params=pltpu.CompilerParams(
            dimension_semantics=("arbitrary", "arbitrary"), vmem_limit_bytes=VMEM_LIMIT_BYTES),
        name="projections",
    )(x, x, *consts, *tables)


def _attn_kernel(q_ref, k_ref, v_ref, o_ref, sc0, sc1, mx0, mx1):
    tq, kv_len = sc0.shape[1:]
    n_items = (q_ref.shape[1] // tq) * N_PAIRS
    sc, mx = (sc0, sc1), (mx0, mx1)
    v_low = lax.broadcasted_iota(jnp.int32, (kv_len, LANES), 1) < V_HALF
    o_low = lax.broadcasted_iota(jnp.int32, (tq, LANES), 1) < V_HALF
    one = jnp.ones((), v_ref.dtype)

    def item(it):
        return pl.multiple_of((it // N_PAIRS) * tq, tq), it % N_PAIRS

    def stage_scores(it, s_dst, m_dst):
        row0, j = item(it)
        mla = j < MLA_PAIRS
        for e in range(2):
            qi = 2 * j + e
            ki = jnp.where(mla, qi, MLA_HEADS + (qi - MLA_HEADS) // GQA_GROUP)
            s = lax.dot_general(q_ref[qi, pl.ds(row0, tq), :], k_ref[ki],
                                (((1,), (1,)), ((), ())), preferred_element_type=jnp.float32)
            s_dst[e] = s
            m_dst[e] = jnp.broadcast_to(jnp.max(s, axis=-1, keepdims=True), (tq, LANES))

    def stage_values(it, s_src, m_src):
        row0, j = item(it)
        vi = jnp.where(j < MLA_PAIRS, j, MLA_PAIRS + (2 * (j - MLA_PAIRS)) // GQA_GROUP)
        v = v_ref[vi]
        accs = []
        for e in range(2):
            m = m_src[e]
            p = jnp.concatenate([jnp.exp2(s_src[e, :, c:c + LANES] - m).astype(v.dtype)
                                 for c in range(0, kv_len, LANES)], axis=-1)
            vh = jnp.where(v_low, v, one) if e == 0 else jnp.where(v_low, one, v)
            accs.append(jnp.dot(p, vh, preferred_element_type=jnp.float32))
        num = jnp.where(o_low, accs[0], accs[1])
        den = jnp.where(o_low, pltpu.roll(accs[0], V_HALF, 1), pltpu.roll(accs[1], V_HALF, 1))
        o_ref[j, pl.ds(row0, tq), :] = (num / den).astype(o_ref.dtype)

    stage_scores(0, sc0, mx0)
    stage_scores(1, sc1, mx1)

    def trip(t, carry):
        stage_values(2 * t, sc0, mx0)
        stage_values(2 * t + 1, sc1, mx1)
        stage_scores(2 * t + 2, sc0, mx0)
        stage_scores(2 * t + 3, sc1, mx1)
        return carry

    lax.fori_loop(0, n_items // 2 - 1, trip, 0)
    stage_values(n_items - 2, sc0, mx0)
    stage_values(n_items - 1, sc1, mx1)


def _attention(q, k, v):
    B, _, S, _ = q.shape
    tb, tq = TQ_BLOCK, TQ_ATTN
    pair_buf = lambda w: pltpu.VMEM((2, tq, w), jnp.float32)
    return pl.pallas_call(
        _attn_kernel,
        grid=(B, S // tb),
        in_specs=[pl.BlockSpec((None, Q_SLOTS, tb, LANES), lambda b, i: (b, 0, i, 0)),
                  pl.BlockSpec((None, K_SLOTS, S, LANES), lambda b, i: (b, 0, 0, 0)),
                  pl.BlockSpec((None, V_SLOTS, S, LANES), lambda b, i: (b, 0, 0, 0))],
        out_specs=pl.BlockSpec((None, N_PAIRS, tb, LANES), lambda b, i: (b, 0, i, 0)),
        out_shape=jax.ShapeDtypeStruct((B, N_PAIRS, S, LANES), jnp.bfloat16),
        scratch_shapes=[pair_buf(S), pair_buf(S), pair_buf(LANES), pair_buf(LANES)],
        compiler_params=pltpu.CompilerParams(
            dimension_semantics=("parallel", "parallel"), vmem_limit_bytes=VMEM_LIMIT_BYTES),
        name="attention",
    )(q, k, v)


def _ffn_kernel(x_ref, o_ref, gain_ref, wo_ref, wg_ref, wu_ref, wd_ref, out_ref):
    oa = jnp.concatenate([o_ref[j] for j in range(MLA_PAIRS)], axis=-1).astype(jnp.float32)
    ob = jnp.concatenate([o_ref[j] for j in range(MLA_PAIRS, N_PAIRS)], axis=-1).astype(jnp.float32)
    na = oa * _rms_scale(oa, oa.shape[-1]) * gain_ref[_G_OUT_A:_G_OUT_A + 1, :oa.shape[-1]]
    nb = ob * _rms_scale(ob, ob.shape[-1]) * gain_ref[_G_OUT_B:_G_OUT_B + 1, :ob.shape[-1]]
    mixed = jnp.concatenate([na, nb], axis=-1)
    x1 = x_ref[...] + _bf16_dot(mixed, wo_ref[...])
    h2 = (x1 * _rms_scale(x1, D_MODEL) * gain_ref[_G_NORM2:_G_NORM2 + 1, :]).astype(jnp.bfloat16)
    g = jnp.dot(h2, wg_ref[...], preferred_element_type=jnp.float32)
    u = jnp.dot(h2, wu_ref[...], preferred_element_type=jnp.float32)
    silu = g / (1.0 + jnp.exp(-g))
    out_ref[...] = x1 + _bf16_dot(silu * u, wd_ref[...])


def _merge_ffn(x, o, consts):
    B, S, _ = x.shape
    tm = TM_FFN
    tok = pl.BlockSpec((None, tm, D_MODEL), lambda b, i: (b, i, 0))
    return pl.pallas_call(
        _ffn_kernel,
        grid=(B, S // tm),
        in_specs=[tok, pl.BlockSpec((None, N_PAIRS, tm, LANES), lambda b, i: (b, 0, i, 0))]
                 + [_const_spec(c.shape) for c in consts],
        out_specs=tok,
        out_shape=jax.ShapeDtypeStruct(x.shape, x.dtype),
        compiler_params=pltpu.CompilerParams(
            dimension_semantics=("parallel", "parallel"), vmem_limit_bytes=VMEM_LIMIT_BYTES),
        name="merge_ffn",
    )(x, o, *consts)


def _rope_tables(seq, d_rot):
    d_ax = d_rot // 2
    inv = np.float32(ROPE_THETA) ** (-(np.arange(0, d_ax, 2, dtype=np.float32) / np.float32(d_ax)))
    rows = seq // GRID_W
    row = np.repeat(np.arange(rows, dtype=np.float32), GRID_W)
    col = np.tile(np.arange(GRID_W, dtype=np.float32), rows)
    ang_r, ang_c = row[:, None] * inv[None, :], col[:, None] * inv[None, :]
    c = np.concatenate([np.cos(ang_r)] * 2 + [np.cos(ang_c)] * 2, axis=-1)
    s = np.concatenate([-np.sin(ang_r), np.sin(ang_r), -np.sin(ang_c), np.sin(ang_c)], axis=-1)
    return c.astype(np.float32), s.astype(np.float32)


def _swap_halves(a):
    n = a.shape[-1] // 4
    return jnp.concatenate([a[..., n:2 * n], a[..., :n], a[..., 3 * n:], a[..., 2 * n:3 * n]], axis=-1)


def _gain_rows(rows):
    rows = [jnp.pad(r.astype(jnp.float32), (0, D_MODEL - r.shape[0])) for r in rows]
    return jnp.pad(jnp.stack(rows), ((0, GAIN_ROWS - len(rows)), (0, 0)))


def kernel(x, norm1_g, w_in, q_a_norm_g, w_q_b, kv_a_norm_g, w_kv_b, mla_q_norm_g, mla_k_norm_g,
           gqa_q_norm_g, gqa_k_norm_g, mla_out_norm_g, gqa_out_norm_g, w_o, norm2_g, w_gate, w_up,
           w_down):
    B, S, _ = x.shape
    bf = jnp.bfloat16
    depth = w_in.shape[0]
    assert S % GRID_W == 0 and S % TB_PROJ == 0 and TB_PROJ % TS_PROJ == 0 and S % TM_FFN == 0
    assert S % TQ_BLOCK == 0 and TQ_BLOCK % TQ_ATTN == 0

    cm, sm = _rope_tables(S, MLA_ROPE)
    cg, sg = _rope_tables(S, GQA_HD)
    ncat = lambda *parts: jnp.asarray(np.concatenate(parts, axis=-1))
    tables = (ncat(np.ones((S, MLA_NOPE), np.float32), cm, sm),
              ncat(cm, sm, cm, sm),
              ncat(cg, sg), ncat(cg, cg), ncat(sg, sg))
    cat = lambda *parts: jnp.concatenate(parts, axis=-1)

    for l in range(depth):
        wi = w_in[l]
        o0 = Q_LORA + KV_LORA
        o1 = o0 + MLA_ROPE
        o2 = o1 + GQA_HEADS * GQA_HD
        o3 = o2 + GQA_KV_HEADS * GQA_HD
        heads = lambda w, n, d: [w[:, h * d:(h + 1) * d] for h in range(n)]
        w_kpe, w_gq, w_gk = wi[:, o0:o1], heads(wi[:, o1:o2], GQA_HEADS, GQA_HD), heads(wi[:, o2:o3], GQA_KV_HEADS, GQA_HD)
        win = cat(
            wi[:, :o0],
            w_kpe, _swap_halves(w_kpe), w_kpe, _swap_halves(w_kpe),
            *[part for w in w_gq for part in (w, _swap_halves(w))],
            *w_gk, *[_swap_halves(w) for w in w_gk],
            wi[:, o3:]).astype(bf)
        wqb = cat(*[part for w in heads(w_q_b[l], MLA_HEADS, MLA_QK)
                    for part in (w, _swap_halves(w[:, MLA_NOPE:]))]).astype(bf)
        wkvb = w_kv_b[l].astype(bf)
        gmq, gmk, ggq, ggk = mla_q_norm_g[l], mla_k_norm_g[l], gqa_q_norm_g[l], gqa_k_norm_g[l]
        gmk_pe, gmk_pe_p, ggk_p = gmk[MLA_NOPE:], _swap_halves(gmk[MLA_NOPE:]), _swap_halves(ggk)
        proj_gains = _gain_rows((
            norm1_g[l], q_a_norm_g[l], kv_a_norm_g[l],
            cat(gmq, _swap_halves(gmq[MLA_NOPE:])), gmk[:MLA_NOPE],
            cat(gmk_pe, gmk_pe_p, gmk_pe, gmk_pe_p),
            cat(ggq, _swap_halves(ggq)), cat(ggk, ggk), cat(ggk_p, ggk_p)))
        proj_consts = (proj_gains, win, wqb, wkvb)
        q, k, v = _projections(x, proj_consts, tables)

        o = _attention(q, k, v)

        ffn_gains = _gain_rows((
            mla_out_norm_g[l], gqa_out_norm_g[l], norm2_g[l]))
        ffn_consts = (ffn_gains, w_o[l].astype(bf), w_gate[l].astype(bf), w_up[l].astype(bf),
                      w_down[l].astype(bf))
        x = _merge_ffn(x, o, ffn_consts)
    return x
```

```python
import math

import jax
import jax.numpy as jnp
import numpy as np
from jax import lax
from jax.experimental import pallas as pl
from jax.experimental.pallas import tpu as pltpu

D_MODEL = 1024
GRID_W = 64
ROPE_THETA = 10000.0
EPS = 1e-6

MLA_HEADS = 8
Q_LORA = 384
KV_LORA = 256
MLA_NOPE = 64
MLA_ROPE = 32
MLA_V = 64
MLA_QK = MLA_NOPE + MLA_ROPE

GQA_HEADS = 8
GQA_KV_HEADS = 2
GQA_HD = 64
GQA_GROUP = GQA_HEADS // GQA_KV_HEADS

D_FF = 2816

LANES = 128
V_HALF = 64
LOG2E = math.log2(math.e)
VMEM_LIMIT_BYTES = 60 * 1024 * 1024

TB_PROJ = 1024
TS_PROJ = 512
TQ_BLOCK = 2048
TQ_ATTN = 1024
TM_FFN = 1024

MLA_PAIRS = MLA_HEADS // 2
N_PAIRS = MLA_PAIRS + GQA_GROUP
Q_SLOTS = MLA_HEADS + GQA_HEADS
K_SLOTS = MLA_HEADS + GQA_KV_HEADS
V_SLOTS = MLA_PAIRS + GQA_KV_HEADS
assert MLA_QK + MLA_ROPE == LANES and MLA_NOPE + MLA_V == LANES and 2 * GQA_HD == LANES
assert MLA_HEADS % 2 == 0 and GQA_GROUP % 2 == 0

_P_CQ = (0, Q_LORA)
_P_CKV = (_P_CQ[1], _P_CQ[1] + KV_LORA)
_P_KPE = (_P_CKV[1], _P_CKV[1] + LANES)
_P_GQ = (_P_KPE[1], _P_KPE[1] + GQA_HEADS * LANES)
_P_GK = (_P_GQ[1], _P_GQ[1] + LANES)
_P_GK_P = (_P_GK[1], _P_GK[1] + LANES)
_P_GV = (_P_GK_P[1], _P_GK_P[1] + LANES)
P_WIDTH = _P_GV[1]

(_G_NORM1, _G_CQ, _G_CKV, _G_MLA_Q, _G_MLA_KN, _G_MLA_KP, _G_GQA_Q, _G_GQA_K, _G_GQA_KP) = range(9)
_G_OUT_A, _G_OUT_B, _G_NORM2 = range(3)
GAIN_ROWS = 16


def _bf16_dot(a, b):
    return jnp.dot(a.astype(jnp.bfloat16), b, preferred_element_type=jnp.float32)


def _rms_scale(x, width):
    return lax.rsqrt(jnp.sum(x * x, axis=-1, keepdims=True) * (1.0 / width) + EPS)


def _proj_kernel(x_ref, xn_ref, gain_ref, win_ref, wqb_ref, wkvb_ref,
                 tmq_ref, tmk_ref, tgq_ref, tgc_ref, tgs_ref,
                 q_ref, k_ref, v_ref, hb0, hb1, ql0, ql1, kv0, kv1, pr0, pr1):
    ts = ql0.shape[0]
    n_items = xn_ref.shape[0] // ts
    assert n_items == 2
    hb, ql, kvs, pr = (hb0, hb1), (ql0, ql1), (kv0, kv1), (pr0, pr1)
    gain = lambda row, width: gain_ref[row:row + 1, :width]
    g1, gcq, gckv = gain(_G_NORM1, D_MODEL), gain(_G_CQ, Q_LORA), gain(_G_CKV, KV_LORA)
    gmq, gmkn, gmkp = gain(_G_MLA_Q, LANES), gain(_G_MLA_KN, LANES), gain(_G_MLA_KP, LANES)
    ggq, ggk, ggkp = gain(_G_GQA_Q, LANES), gain(_G_GQA_K, LANES), gain(_G_GQA_KP, LANES)

    def rows(it):
        return pl.ds(it * ts, ts)

    def stage_norm(src_ref, it, h_dst):
        x = src_ref[rows(it), :]
        h_dst[...] = (x * _rms_scale(x, D_MODEL) * g1).astype(h_dst.dtype)

    def stage_matmuls(h_src, ql_dst, kv_dst, pr_dst):
        p = jnp.dot(h_src[...], win_ref[...], preferred_element_type=jnp.float32)
        cq = p[:, _P_CQ[0]:_P_CQ[1]]
        ckv = p[:, _P_CKV[0]:_P_CKV[1]]
        ql_dst[...] = _bf16_dot(cq * _rms_scale(cq, Q_LORA) * gcq, wqb_ref[...])
        kv_dst[...] = _bf16_dot(ckv * _rms_scale(ckv, KV_LORA) * gckv, wkvb_ref[...])
        pr_dst[...] = p[:, _P_KPE[0]:]

    def stage_heads(it, ql_src, kv_src, pr_src):
        r = rows(it)
        off = lambda span: slice(span[0] - _P_KPE[0], span[1] - _P_KPE[0])
        lane = lax.broadcasted_iota(jnp.int32, (ts, LANES), 1)
        low = lane < LANES // 2

        y = pr_src[:, off(_P_KPE)] * gmkp * tmk_ref[r, :]
        kpe_rot = y + pltpu.roll(y, MLA_ROPE, 1)
        kpe = jnp.where(lane < MLA_ROPE, pr_src[:, off(_P_KPE)], 0.0)
        ss_pe = jnp.sum(kpe * kpe, axis=-1, keepdims=True)
        tmq = tmq_ref[r, :] * (gmq * (MLA_QK ** -0.5 * LOG2E))
        for hh in range(MLA_HEADS):
            sl = slice(hh * LANES, (hh + 1) * LANES)
            qh = ql_src[:, sl]
            qsq = jnp.where(lane < MLA_QK, qh * qh, 0.0)
            rq = lax.rsqrt(jnp.sum(qsq, axis=-1, keepdims=True) * (1.0 / MLA_QK) + EPS)
            q_ref[hh, r, :] = (qh * rq * tmq).astype(q_ref.dtype)
            kn = kv_src[:, sl]
            ss_kn = jnp.sum(jnp.where(low, kn * kn, 0.0), axis=-1, keepdims=True)
            rk = lax.rsqrt((ss_kn + ss_pe) * (1.0 / MLA_QK) + EPS)
            k_ref[hh, r, :] = (jnp.where(low, kn * gmkn, kpe_rot) * rk).astype(k_ref.dtype)
        for j in range(MLA_PAIRS):
            even = kv_src[:, 2 * j * LANES:(2 * j + 1) * LANES]
            odd = kv_src[:, (2 * j + 1) * LANES:(2 * j + 2) * LANES]
            v_ref[j, r, :] = jnp.where(low, pltpu.roll(even, MLA_V, 1), odd).astype(v_ref.dtype)

        tgq = tgq_ref[r, :] * (ggq * (GQA_HD ** -0.5 * LOG2E))
        gq0 = off(_P_GQ).start
        for hh in range(GQA_HEADS):
            qh = pr_src[:, gq0 + hh * LANES:gq0 + (hh + 1) * LANES]
            q_ref[MLA_HEADS + hh, r, :] = (qh * _rms_scale(qh, LANES) * tgq).astype(q_ref.dtype)
        gk, gk_p = pr_src[:, off(_P_GK)], pr_src[:, off(_P_GK_P)]
        sq = gk * gk
        ss_lo = jnp.sum(jnp.where(low, sq, 0.0), axis=-1, keepdims=True)
        ss_hi = jnp.sum(jnp.where(low, 0.0, sq), axis=-1, keepdims=True)
        rk = jnp.where(low, lax.rsqrt(ss_lo * (1.0 / GQA_HD) + EPS), lax.rsqrt(ss_hi * (1.0 / GQA_HD) + EPS))
        rot = (gk * (ggk * tgc_ref[r, :]) + gk_p * (ggkp * tgs_ref[r, :])) * rk
        rot_x = pltpu.roll(rot, GQA_HD, 1)
        k_ref[MLA_HEADS, r, :] = jnp.where(low, rot, rot_x).astype(k_ref.dtype)
        k_ref[MLA_HEADS + 1, r, :] = jnp.where(low, rot_x, rot).astype(k_ref.dtype)
        gv = pr_src[:, off(_P_GV)]
        gv_x = pltpu.roll(gv, GQA_HD, 1)
        v_ref[MLA_PAIRS, r, :] = jnp.where(low, gv, gv_x).astype(v_ref.dtype)
        v_ref[MLA_PAIRS + 1, r, :] = jnp.where(low, gv_x, gv).astype(v_ref.dtype)

    @pl.when((pl.program_id(0) == 0) & (pl.program_id(1) == 0))
    def _():
        stage_norm(x_ref, 0, hb0)
        stage_norm(x_ref, 1, hb1)
        stage_matmuls(hb0, ql0, kv0, pr0)

    for it in range(n_items):
        cur = it % 2
        stage_norm(xn_ref, it, hb[cur])
        stage_matmuls(hb[1 - cur], ql[1 - cur], kvs[1 - cur], pr[1 - cur])
        stage_heads(it, ql[cur], kvs[cur], pr[cur])


def _const_spec(shape):
    return pl.BlockSpec(shape, lambda *_: (0,) * len(shape), pipeline_mode=pl.Buffered(1))


def _projections(x, consts, tables):
    B, S, _ = x.shape
    tb, ts = TB_PROJ, TS_PROJ
    slots = lambda n: pl.BlockSpec((None, n, tb, LANES), lambda b, i: (b, 0, i, 0))
    tab = pl.BlockSpec((tb, LANES), lambda b, i: (i, 0))
    n_slots = (Q_SLOTS, K_SLOTS, V_SLOTS)
    widths = (MLA_HEADS * LANES, MLA_HEADS * (MLA_NOPE + MLA_V), P_WIDTH - _P_KPE[0])
    n_i = S // tb

    def next_block(b, i):
        t = jnp.minimum(b * n_i + i + 1, B * n_i - 1)
        return t // n_i, t % n_i, 0

    return pl.pallas_call(
        _proj_kernel,
        grid=(B, n_i),
        in_specs=[pl.BlockSpec((None, tb, D_MODEL), lambda b, i: (0, 0, 0), pipeline_mode=pl.Buffered(1)),
                  pl.BlockSpec((None, tb, D_MODEL), next_block)]
                 + [_const_spec(c.shape) for c in consts] + [tab] * len(tables),
        out_specs=[slots(n) for n in n_slots],
        out_shape=[jax.ShapeDtypeStruct((B, n, S, LANES), jnp.bfloat16) for n in n_slots],
        scratch_shapes=[pltpu.VMEM((ts, D_MODEL), jnp.bfloat16)] * 2
                       + [pltpu.VMEM((ts, w), jnp.float32) for w in widths for _ in range(2)],
        compiler_params=pltpu.CompilerParams(
            dimension_semantics=("arbitrary", "arbitrary"), vmem_limit_bytes=VMEM_LIMIT_BYTES),
        name="projections",
    )(x, x, *consts, *tables)


def _attn_kernel(q_ref, k_ref, v_ref, o_ref, sc0, mx0):
    tq, kv_len = sc0.shape[1:]
    n_items = (q_ref.shape[1] // tq) * N_PAIRS
    v_low = lax.broadcasted_iota(jnp.int32, (kv_len, LANES), 1) < V_HALF
    o_low = lax.broadcasted_iota(jnp.int32, (tq, LANES), 1) < V_HALF
    one = jnp.ones((), v_ref.dtype)

    def item(it):
        return pl.multiple_of((it // N_PAIRS) * tq, tq), it % N_PAIRS

    def stage_scores(it, s_dst, m_dst):
        row0, j = item(it)
        mla = j < MLA_PAIRS
        for e in range(2):
            qi = 2 * j + e
            ki = jnp.where(mla, qi, MLA_HEADS + (qi - MLA_HEADS) // GQA_GROUP)
            s = lax.dot_general(q_ref[qi, pl.ds(row0, tq), :], k_ref[ki],
                                (((1,), (1,)), ((), ())), preferred_element_type=jnp.float32)
            s_dst[e] = s
            m_dst[e] = jnp.broadcast_to(jnp.max(s, axis=-1, keepdims=True), (tq, LANES))

    def stage_values(it, s_src, m_src):
        row0, j = item(it)
        vi = jnp.where(j < MLA_PAIRS, j, MLA_PAIRS + (2 * (j - MLA_PAIRS)) // GQA_GROUP)
        v = v_ref[vi]
        accs = []
        for e in range(2):
            m = m_src[e]
            p = jnp.concatenate([jnp.exp2(s_src[e, :, c:c + LANES] - m).astype(v.dtype)
                                 for c in range(0, kv_len, LANES)], axis=-1)
            vh = jnp.where(v_low, v, one) if e == 0 else jnp.where(v_low, one, v)
            accs.append(jnp.dot(p, vh, preferred_element_type=jnp.float32))
        num = jnp.where(o_low, accs[0], accs[1])
        den = jnp.where(o_low, pltpu.roll(accs[0], V_HALF, 1), pltpu.roll(accs[1], V_HALF, 1))
        o_ref[j, pl.ds(row0, tq), :] = (num / den).astype(o_ref.dtype)

    stage_scores(0, sc0, mx0)

    def trip(t, carry):
        stage_values(t, sc0, mx0)
        stage_scores(t + 1, sc0, mx0)
        return carry

    lax.fori_loop(0, n_items - 1, trip, 0)
    stage_values(n_items - 1, sc0, mx0)


def _attention(q, k, v):
    B, _, S, _ = q.shape
    tb, tq = TQ_BLOCK, TQ_ATTN
    pair_buf = lambda w: pltpu.VMEM((2, tq, w), jnp.float32)
    return pl.pallas_call(
        _attn_kernel,
        grid=(B, S // tb),
        in_specs=[pl.BlockSpec((None, Q_SLOTS, tb, LANES), lambda b, i: (b, 0, i, 0)),
                  pl.BlockSpec((None, K_SLOTS, S, LANES), lambda b, i: (b, 0, 0, 0)),
                  pl.BlockSpec((None, V_SLOTS, S, LANES), lambda b, i: (b, 0, 0, 0))],
        out_specs=pl.BlockSpec((None, N_PAIRS, tb, LANES), lambda b, i: (b, 0, i, 0)),
        out_shape=jax.ShapeDtypeStruct((B, N_PAIRS, S, LANES), jnp.bfloat16),
        scratch_shapes=[pair_buf(S), pair_buf(LANES)],
        compiler_params=pltpu.CompilerParams(
            dimension_semantics=("parallel", "parallel"), vmem_limit_bytes=VMEM_LIMIT_BYTES),
        name="attention",
    )(q, k, v)


def _ffn_kernel(x_ref, o_ref, gain_ref, wo_ref, wg_ref, wu_ref, wd_ref, out_ref):
    oa = jnp.concatenate([o_ref[j] for j in range(MLA_PAIRS)], axis=-1).astype(jnp.float32)
    ob = jnp.concatenate([o_ref[j] for j in range(MLA_PAIRS, N_PAIRS)], axis=-1).astype(jnp.float32)
    na = oa * _rms_scale(oa, oa.shape[-1]) * gain_ref[_G_OUT_A:_G_OUT_A + 1, :oa.shape[-1]]
    nb = ob * _rms_scale(ob, ob.shape[-1]) * gain_ref[_G_OUT_B:_G_OUT_B + 1, :ob.shape[-1]]
    mixed = jnp.concatenate([na, nb], axis=-1)
    x1 = x_ref[...] + _bf16_dot(mixed, wo_ref[...])
    h2 = (x1 * _rms_scale(x1, D_MODEL) * gain_ref[_G_NORM2:_G_NORM2 + 1, :]).astype(jnp.bfloat16)
    g = jnp.dot(h2, wg_ref[...], preferred_element_type=jnp.float32)
    u = jnp.dot(h2, wu_ref[...], preferred_element_type=jnp.float32)
    silu = g / (1.0 + jnp.exp(-g))
    out_ref[...] = x1 + _bf16_dot(silu * u, wd_ref[...])


def _merge_ffn(x, o, consts):
    B, S, _ = x.shape
    tm = TM_FFN
    tok = pl.BlockSpec((None, tm, D_MODEL), lambda b, i: (b, i, 0))
    return pl.pallas_call(
        _ffn_kernel,
        grid=(B, S // tm),
        in_specs=[tok, pl.BlockSpec((None, N_PAIRS, tm, LANES), lambda b, i: (b, 0, i, 0))]
                 + [_const_spec(c.shape) for c in consts],
        out_specs=tok,
        out_shape=jax.ShapeDtypeStruct(x.shape, x.dtype),
        compiler_params=pltpu.CompilerParams(
            dimension_semantics=("parallel", "parallel"), vmem_limit_bytes=VMEM_LIMIT_BYTES),
        name="merge_ffn",
    )(x, o, *consts)


def _rope_tables(seq, d_rot):
    d_ax = d_rot // 2
    inv = np.float32(ROPE_THETA) ** (-(np.arange(0, d_ax, 2, dtype=np.float32) / np.float32(d_ax)))
    rows = seq // GRID_W
    row = np.repeat(np.arange(rows, dtype=np.float32), GRID_W)
    col = np.tile(np.arange(GRID_W, dtype=np.float32), rows)
    ang_r, ang_c = row[:, None] * inv[None, :], col[:, None] * inv[None, :]
    c = np.concatenate([np.cos(ang_r)] * 2 + [np.cos(ang_c)] * 2, axis=-1)
    s = np.concatenate([-np.sin(ang_r), np.sin(ang_r), -np.sin(ang_c), np.sin(ang_c)], axis=-1)
    return c.astype(np.float32), s.astype(np.float32)


def _swap_halves(a):
    n = a.shape[-1] // 4
    return jnp.concatenate([a[..., n:2 * n], a[..., :n], a[..., 3 * n:], a[..., 2 * n:3 * n]], axis=-1)


def _gain_rows(rows):
    rows = [jnp.pad(r.astype(jnp.float32), (0, D_MODEL - r.shape[0])) for r in rows]
    return jnp.pad(jnp.stack(rows), ((0, GAIN_ROWS - len(rows)), (0, 0)))


def kernel(x, norm1_g, w_in, q_a_norm_g, w_q_b, kv_a_norm_g, w_kv_b, mla_q_norm_g, mla_k_norm_g,
           gqa_q_norm_g, gqa_k_norm_g, mla_out_norm_g, gqa_out_norm_g, w_o, norm2_g, w_gate, w_up,
           w_down):
    B, S, _ = x.shape
    bf = jnp.bfloat16
    depth = w_in.shape[0]
    assert S % GRID_W == 0 and S % TB_PROJ == 0 and TB_PROJ % TS_PROJ == 0 and S % TM_FFN == 0
    assert S % TQ_BLOCK == 0 and TQ_BLOCK % TQ_ATTN == 0

    cm, sm = _rope_tables(S, MLA_ROPE)
    cg, sg = _rope_tables(S, GQA_HD)
    ncat = lambda *parts: jnp.asarray(np.concatenate(parts, axis=-1))
    tables = (ncat(np.ones((S, MLA_NOPE), np.float32), cm, sm),
              ncat(cm, sm, cm, sm),
              ncat(cg, sg), ncat(cg, cg), ncat(sg, sg))
    cat = lambda *parts: jnp.concatenate(parts, axis=-1)

    for l in range(depth):
        wi = w_in[l]
        o0 = Q_LORA + KV_LORA
        o1 = o0 + MLA_ROPE
        o2 = o1 + GQA_HEADS * GQA_HD
        o3 = o2 + GQA_KV_HEADS * GQA_HD
        w_kpe, w_gq = wi[:, o0:o1], wi[:, o1:o2].reshape(D_MODEL, GQA_HEADS, GQA_HD)
        w_gk = wi[:, o2:o3].reshape(D_MODEL, GQA_KV_HEADS, GQA_HD)
        win = cat(
            wi[:, :o0],
            w_kpe, _swap_halves(w_kpe), w_kpe, _swap_halves(w_kpe),
            cat(w_gq, _swap_halves(w_gq)).reshape(D_MODEL, GQA_HEADS * LANES),
            w_gk.reshape(D_MODEL, LANES), _swap_halves(w_gk).reshape(D_MODEL, LANES),
            wi[:, o3:]).astype(bf)
        wq = w_q_b[l].reshape(Q_LORA, MLA_HEADS, MLA_QK)
        wqb = cat(wq, _swap_halves(wq[..., MLA_NOPE:])).reshape(Q_LORA, MLA_HEADS * LANES).astype(bf)
        wkvb = w_kv_b[l].astype(bf)
        gmq, gmk, ggq, ggk = mla_q_norm_g[l], mla_k_norm_g[l], gqa_q_norm_g[l], gqa_k_norm_g[l]
        gmk_pe, gmk_pe_p, ggk_p = gmk[MLA_NOPE:], _swap_halves(gmk[MLA_NOPE:]), _swap_halves(ggk)
        proj_gains = _gain_rows((
            norm1_g[l], q_a_norm_g[l], kv_a_norm_g[l],
            cat(gmq, _swap_halves(gmq[MLA_NOPE:])), gmk[:MLA_NOPE],
            cat(gmk_pe, gmk_pe_p, gmk_pe, gmk_pe_p),
            cat(ggq, _swap_halves(ggq)), cat(ggk, ggk), cat(ggk_p, ggk_p)))
        proj_consts = (proj_gains, win, wqb, wkvb)
        q, k, v = _projections(x, proj_consts, tables)

        o = _attention(q, k, v)

        ffn_gains = _gain_rows((
            mla_out_norm_g[l], gqa_out_norm_g[l], norm2_g[l]))
        ffn_consts = (ffn_gains, w_o[l].astype(bf), w_gate[l].astype(bf), w_up[l].astype(bf),
                      w_down[l].astype(bf))
        x = _merge_ffn(x, o, ffn_consts)
    return x
```
